```python
import math
import jax
import jax.numpy as jnp
from jax import lax
import numpy as np

D_MODEL = 1024
BATCH = 32
SEQ = 2048
DEPTH = 4

CHUNK = 64
N_BRANCH = 4
BRANCH_WIDTH = D_MODEL // N_BRANCH
RET_HEADS = 4
RET_DV = BRANCH_WIDTH // RET_HEADS
RET_DK = RET_DV // 2
ATT_HEADS = 4
ATT_DH = BRANCH_WIDTH // ATT_HEADS
ATT_LEFT_CHUNKS = 8
ATT_MAX_REL = 128
GLA_HEADS = 4
GLA_DV = BRANCH_WIDTH // GLA_HEADS
GLA_DK = GLA_DV // 2
GLA_GATE_RANK = 16
GLA_GATE_TAU = 16.0
S5_GROUP = 16
S5_GROUPS = BRANCH_WIDTH // S5_GROUP
S5_STATE = 64
D_FF = ((8 * D_MODEL // 3 + 255) // 256) * 256
DEEPNORM_ALPHA = (2.0 * DEPTH) ** 0.25
DEEPNORM_BETA = (8.0 * DEPTH) ** -0.25
LN_EPS = 1e-5
IN_SIZES = (
    RET_HEADS * RET_DK, RET_HEADS * RET_DK, RET_HEADS * RET_DV, RET_HEADS * RET_DV,
    ATT_HEADS * ATT_DH, ATT_HEADS * ATT_DH, ATT_HEADS * ATT_DH,
    GLA_HEADS * GLA_DK, GLA_HEADS * GLA_DK, GLA_HEADS * GLA_DV, GLA_HEADS * GLA_DV,
    GLA_GATE_RANK,
    BRANCH_WIDTH,
)
IN_WIDTH = sum(IN_SIZES)

kernel_name = "hybrid_gated_streaming_encoder"


def _split_cols(h, sizes):
    out, start = [], 0
    for s in sizes:
        out.append(h[..., start:start + s])
        start += s
    return out


def _layer_norm(x, g, b):
    xf = x.astype(jnp.float32)
    mu = jnp.mean(xf, axis=-1, keepdims=True)
    var = jnp.mean(jnp.square(xf - mu), axis=-1, keepdims=True)
    y = (xf - mu) * lax.rsqrt(var + LN_EPS) * g.astype(jnp.float32) + b.astype(jnp.float32)
    return y.astype(x.dtype)


def _head_norm(o):
    mu = jnp.mean(o, axis=-1, keepdims=True)
    var = jnp.mean(jnp.square(o - mu), axis=-1, keepdims=True)
    return (o - mu) * lax.rsqrt(var + LN_EPS)


def _rotary(x, pos):
    half = x.shape[-1] // 2
    inv = 1.0 / (10000.0 ** (jnp.arange(half, dtype=jnp.float32) / half))
    ang = pos.astype(jnp.float32)[:, None] * inv[None, :]
    cos = jnp.cos(ang)[None, :, None, :]
    sin = jnp.sin(ang)[None, :, None, :]
    x1, x2 = x[..., :half], x[..., half:]
    return jnp.concatenate([x1 * cos - x2 * sin, x1 * sin + x2 * cos], axis=-1)


def _retention(q, k, v, g):
    bsz, seq = q.shape[:2]
    nc = seq // CHUNK
    pos = jnp.arange(seq)
    qf = _rotary(q.astype(jnp.float32).reshape(bsz, seq, RET_HEADS, RET_DK), pos)
    kf = _rotary(k.astype(jnp.float32).reshape(bsz, seq, RET_HEADS, RET_DK), pos) * RET_DK ** -0.5
    vf = v.astype(jnp.float32)
    log_g = jnp.log(1.0 - 2.0 ** (-5.0 - jnp.arange(RET_HEADS, dtype=jnp.float32)))
    j = jnp.arange(CHUNK, dtype=jnp.float32)
    d_intra = jnp.exp(log_g[:, None, None] * jnp.abs(j[:, None] - j[None, :]))
    xi = jnp.exp(log_g[None, :] * (j[:, None] + 1.0))
    zeta = jnp.exp(log_g[None, :] * (CHUNK - 1.0 - j[:, None]))
    decay_chunk = jnp.exp(log_g * CHUNK)
    qc = qf.reshape(bsz, nc, CHUNK, RET_HEADS, RET_DK)
    kc = kf.reshape(bsz, nc, CHUNK, RET_HEADS, RET_DK)
    vc = vf.reshape(bsz, nc, CHUNK, RET_HEADS, RET_DV)
    s = jnp.einsum('bnqhd,bnkhd->bnhqk', qc, kc) * d_intra
    o_intra = jnp.einsum('bnhqk,bnkhe->bnqhe', s, vc)
    upd = jnp.einsum('bnkhd,bnkhe->bnhde', kc * zeta[None, None, :, :, None], vc)

    def step(state, u):
        return decay_chunk[None, :, None, None] * state + u, state

    init = jnp.zeros((bsz, RET_HEADS, RET_DK, RET_DV), jnp.float32)
    _, r_prev = lax.scan(step, init, jnp.moveaxis(upd, 1, 0))
    r_prev = jnp.moveaxis(r_prev, 0, 1)
    o_cross = jnp.einsum('bnqhd,bnhde->bnqhe', qc * xi[None, None, :, :, None], r_prev)
    o = _head_norm(o_intra + o_cross).reshape(bsz, seq, RET_HEADS * RET_DV)
    return (jax.nn.silu(g.astype(jnp.float32)) * o).astype(q.dtype)


def _chunk_attention(q, k, v, rel_bias):
    bsz, seq = q.shape[:2]
    nc = seq // CHUNK
    left = ATT_LEFT_CHUNKS * CHUNK
    band = left + CHUNK
    qf = q.astype(jnp.float32).reshape(bsz, seq, ATT_HEADS, ATT_DH) * ATT_DH ** -0.5
    kp = jnp.pad(k.astype(jnp.float32).reshape(bsz, seq, ATT_HEADS, ATT_DH), ((0, 0), (left, 0), (0, 0), (0, 0)))
    vp = jnp.pad(v.astype(jnp.float32).reshape(bsz, seq, ATT_HEADS, ATT_DH), ((0, 0), (left, 0), (0, 0), (0, 0)))
    jq = jnp.arange(CHUNK)[:, None]
    pk = jnp.arange(band)[None, :]
    rel = jnp.clip(left + jq - pk, -ATT_MAX_REL, ATT_MAX_REL) + ATT_MAX_REL
    bias = rel_bias.astype(jnp.float32)[:, rel]

    def one_chunk(i):
        start = i * CHUNK
        qi = lax.dynamic_slice_in_dim(qf, start, CHUNK, axis=1)
        ki = lax.dynamic_slice_in_dim(kp, start, band, axis=1)
        vi = lax.dynamic_slice_in_dim(vp, start, band, axis=1)
        s = jnp.einsum('bqhd,bkhd->bhqk', qi, ki) + bias[None]
        valid = (start - left + jnp.arange(band)) >= 0
        s = jnp.where(valid[None, None, None, :], s, -1e30)
        p = jax.nn.softmax(s, axis=-1)
        return jnp.einsum('bhqk,bkhd->bqhd', p, vi)

    o = lax.map(one_chunk, jnp.arange(nc))
    return jnp.moveaxis(o, 0, 1).reshape(bsz, seq, ATT_HEADS * ATT_DH).astype(q.dtype)


def _gla(q, k, v, r, a_lr, w_a_up, b_a):
    bsz, seq = q.shape[:2]
    nc = seq // CHUNK
    shp_k = (bsz, nc, CHUNK, GLA_HEADS, GLA_DK)
    qc = q.astype(jnp.float32).reshape(shp_k) * GLA_DK ** -0.5
    kc = k.astype(jnp.float32).reshape(shp_k)
    vc = v.astype(jnp.float32).reshape(bsz, nc, CHUNK, GLA_HEADS, GLA_DV)
    z = a_lr.astype(jnp.float32) @ w_a_up.astype(jnp.float32) + b_a.astype(jnp.float32)
    log_a = (jax.nn.log_sigmoid(z) / GLA_GATE_TAU).reshape(shp_k)
    cum = jnp.cumsum(log_a, axis=2)
    last = cum[:, :, -1:]
    k_dec = kc * jnp.exp(last - cum)
    upd = jnp.einsum('bnchk,bnchv->bnhkv', k_dec, vc)
    g_chunk = jnp.exp(last[:, :, 0])

    def combine(a, b):
        ga, ua = a
        gb, ub = b
        return ga * gb, gb[..., None] * ua + ub

    _, states = lax.associative_scan(combine, (g_chunk, upd), axis=1)
    o = jnp.einsum('bnchk,bnhkv->bnchv', qc, states)
    o = _head_norm(o).reshape(bsz, seq, GLA_HEADS * GLA_DV)
    return (jax.nn.silu(r.astype(jnp.float32)) * o).astype(q.dtype)


def _s5(u, lam_re, lam_im, log_dt, b_re, b_im, c_re, c_im, d_skip, w_glu, b_glu):
    bsz, seq = u.shape[:2]
    f32 = jnp.float32
    uf = u.astype(f32).reshape(bsz, seq, S5_GROUPS, S5_GROUP)
    lr, li = lam_re.astype(f32), lam_im.astype(f32)
    dt = jnp.exp(log_dt.astype(f32))[:, None]
    mag = jnp.exp(lr * dt)
    ab_re, ab_im = mag * jnp.cos(li * dt), mag * jnp.sin(li * dt)
    den = lr * lr + li * li
    nr, ni = ab_re - 1.0, ab_im
    coef_re = (nr * lr + ni * li) / den
    coef_im = (ni * lr - nr * li) / den
    br, bi = b_re.astype(f32), b_im.astype(f32)
    bb_re = coef_re[..., None] * br - coef_im[..., None] * bi
    bb_im = coef_re[..., None] * bi + coef_im[..., None] * br
    bu_re = jnp.einsum('bsgi,gpi->bsgp', uf, bb_re)
    bu_im = jnp.einsum('bsgi,gpi->bsgp', uf, bb_im)
    a_re = jnp.broadcast_to(ab_re, bu_re.shape)
    a_im = jnp.broadcast_to(ab_im, bu_im.shape)

    def combine(e1, e2):
        a1r, a1i, b1r, b1i = e1
        a2r, a2i, b2r, b2i = e2
        return (a2r * a1r - a2i * a1i, a2r * a1i + a2i * a1r,
                a2r * b1r - a2i * b1i + b2r, a2r * b1i + a2i * b1r + b2i)

    _, _, xr, xim = lax.associative_scan(combine, (a_re, a_im, bu_re, bu_im), axis=1)
    y = jnp.einsum('gip,bsgp->bsgi', c_re.astype(f32), xr) - jnp.einsum('gip,bsgp->bsgi', c_im.astype(f32), xim)
    y = y.reshape(bsz, seq, BRANCH_WIDTH) + d_skip.astype(f32) * uf.reshape(bsz, seq, BRANCH_WIDTH)
    y = jax.nn.gelu(y)
    y = y * jax.nn.sigmoid(y @ w_glu.astype(f32) + b_glu.astype(f32))
    return y.astype(u.dtype)


def setup_inputs(seed: int = 0) -> dict:
    key = jax.random.key(seed)
    ks = jax.random.split(key, 28)
    f32 = jnp.float32

    def nrm(k, shape, scale):
        return jax.random.normal(k, shape, f32) * scale

    n_idx = jnp.arange(S5_STATE, dtype=f32)
    gp = (DEPTH, S5_GROUPS, S5_STATE)
    return {
        'x': nrm(ks[0], (BATCH, SEQ, D_MODEL), 1.0),
        'w_in': nrm(ks[1], (DEPTH, D_MODEL, IN_WIDTH), D_MODEL ** -0.5),
        'gla_w_a': nrm(ks[2], (DEPTH, GLA_GATE_RANK, GLA_HEADS * GLA_DK), GLA_GATE_RANK ** -0.5),
        'gla_b_a': nrm(ks[3], (DEPTH, GLA_HEADS * GLA_DK), 0.1),
        'att_rel_bias': nrm(ks[4], (DEPTH, ATT_HEADS, 2 * ATT_MAX_REL + 1), 0.1),
        's5_lambda_re': -0.5 + nrm(ks[5], gp, 0.01),
        's5_lambda_im': math.pi * n_idx + nrm(ks[6], gp, 0.01),
        's5_log_dt': jax.random.uniform(ks[7], (DEPTH, S5_GROUPS), f32, math.log(1e-3), math.log(1e-1)),
        's5_b_re': nrm(ks[8], (DEPTH, S5_GROUPS, S5_STATE, S5_GROUP), (2.0 * S5_GROUP) ** -0.5),
        's5_b_im': nrm(ks[9], (DEPTH, S5_GROUPS, S5_STATE, S5_GROUP), (2.0 * S5_GROUP) ** -0.5),
        's5_c_re': nrm(ks[10], (DEPTH, S5_GROUPS, S5_GROUP, S5_STATE), S5_STATE ** -0.5),
        's5_c_im': nrm(ks[11], (DEPTH, S5_GROUPS, S5_GROUP, S5_STATE), S5_STATE ** -0.5),
        's5_d': nrm(ks[12], (DEPTH, BRANCH_WIDTH), 1.0),
        's5_w_glu': nrm(ks[13], (DEPTH, BRANCH_WIDTH, BRANCH_WIDTH), BRANCH_WIDTH ** -0.5),
        's5_b_glu': nrm(ks[14], (DEPTH, BRANCH_WIDTH), 0.02),
        'w_gate': nrm(ks[15], (DEPTH, N_BRANCH, D_MODEL, D_MODEL), D_MODEL ** -0.5),
        'b_gate': nrm(ks[16], (DEPTH, N_BRANCH, D_MODEL), 0.02),
        'w_branch': nrm(ks[17], (DEPTH, N_BRANCH, BRANCH_WIDTH, D_MODEL), BRANCH_WIDTH ** -0.5),
        'w_out': nrm(ks[18], (DEPTH, D_MODEL, D_MODEL), D_MODEL ** -0.5 * DEEPNORM_BETA),
        'ln1_g': 1.0 + nrm(ks[19], (DEPTH, D_MODEL), 0.02),
        'ln1_b': nrm(ks[20], (DEPTH, D_MODEL), 0.02),
        'w_ffn_gate': nrm(ks[21], (DEPTH, D_MODEL, D_FF), D_MODEL ** -0.5),
        'w_ffn_up': nrm(ks[22], (DEPTH, D_MODEL, D_FF), D_MODEL ** -0.5),
        'w_ffn_down': nrm(ks[23], (DEPTH, D_FF, D_MODEL), D_FF ** -0.5 * DEEPNORM_BETA),
        'ln2_g': 1.0 + nrm(ks[24], (DEPTH, D_MODEL), 0.02),
        'ln2_b': nrm(ks[25], (DEPTH, D_MODEL), 0.02),
    }


def reference(x, w_in, gla_w_a, gla_b_a, att_rel_bias, s5_lambda_re, s5_lambda_im, s5_log_dt,
              s5_b_re, s5_b_im, s5_c_re, s5_c_im, s5_d, s5_w_glu, s5_b_glu, w_gate, b_gate,
              w_branch, w_out, ln1_g, ln1_b, w_ffn_gate, w_ffn_up, w_ffn_down, ln2_g, ln2_b):
    bsz, seq = x.shape[:2]
    for l in range(DEPTH):
        h = x @ w_in[l]
        (rq, rk, rv, rg, aq, ak, av, gq, gk, gv, gr, ga, su) = _split_cols(h, IN_SIZES)
        o_ret = _retention(rq, rk, rv, rg)
        o_att = _chunk_attention(aq, ak, av, att_rel_bias[l])
        o_gla = _gla(gq, gk, gv, gr, ga, gla_w_a[l], gla_b_a[l])
        o_s5 = _s5(su, s5_lambda_re[l], s5_lambda_im[l], s5_log_dt[l], s5_b_re[l], s5_b_im[l],
                   s5_c_re[l], s5_c_im[l], s5_d[l], s5_w_glu[l], s5_b_glu[l])
        branches = (o_ret, o_att, o_gla, o_s5)
        mixed = jax.nn.sigmoid(x @ w_gate[l, 0] + b_gate[l, 0]) * (branches[0] @ w_branch[l, 0])
        for bi in range(1, N_BRANCH):
            gate = jax.nn.sigmoid(x @ w_gate[l, bi] + b_gate[l, bi])
            mixed = mixed + gate * (branches[bi] @ w_branch[l, bi])
        x = _layer_norm(DEEPNORM_ALPHA * x + mixed @ w_out[l], ln1_g[l], ln1_b[l])
        ffn = (jax.nn.silu(x @ w_ffn_gate[l]) * (x @ w_ffn_up[l])) @ w_ffn_down[l]
        x = _layer_norm(DEEPNORM_ALPHA * x + ffn, ln2_g[l], ln2_b[l])
    return x
```

```python
import functools
import math

import jax
import jax.numpy as jnp
from jax import lax
from jax.experimental import pallas as pl
from jax.experimental.pallas import tpu as pltpu

F32 = jnp.float32
BF16 = jnp.bfloat16

D_MODEL = 1024
DEPTH = 4
CHUNK = 64
N_BRANCH = 4
BRANCH_WIDTH = D_MODEL // N_BRANCH
RET_HEADS = 4
RET_DV = BRANCH_WIDTH // RET_HEADS
RET_DK = RET_DV // 2
ATT_HEADS = 4
ATT_DH = BRANCH_WIDTH // ATT_HEADS
ATT_LEFT_CHUNKS = 8
ATT_MAX_REL = 128
GLA_HEADS = 4
GLA_DV = BRANCH_WIDTH // GLA_HEADS
GLA_DK = GLA_DV // 2
GLA_GATE_RANK = 16
GLA_GATE_TAU = 16.0
S5_GROUP = 16
S5_GROUPS = BRANCH_WIDTH // S5_GROUP
S5_STATE = 64
D_FF = ((8 * D_MODEL // 3 + 255) // 256) * 256
DEEPNORM_ALPHA = (2.0 * DEPTH) ** 0.25
LN_EPS = 1e-5

_IN_SIZES = (128, 128, 256, 256, 256, 256, 256, 128, 128, 256, 256, GLA_GATE_RANK, 256)
_IN_OFF = tuple(sum(_IN_SIZES[:i]) for i in range(len(_IN_SIZES)))

RET_W = 1024
ATT_W = 768
GLA_W = 896
S5_W = 256
LANE = 128
QK_W = RET_HEADS * RET_DK

BLK = 4 * CHUNK
ATT_LEFT = ATT_LEFT_CHUNKS * CHUNK
ATT_BAND = ATT_LEFT + CHUNK
S5_NB = 8
S5_NS = S5_GROUPS * S5_STATE
VMEM_LIMIT = 56 * 1024 * 1024


def _dot(a, b):
    return jnp.dot(a, b, preferred_element_type=F32)


def _dot_nt(a, b):
    return lax.dot_general(a, b, (((1,), (1,)), ((), ())), preferred_element_type=F32)


def _split_dot(a, b):
    hi = a.astype(BF16)
    lo = (a - hi.astype(F32)).astype(BF16)
    return _dot(hi, b) + _dot(lo, b)


def _layer_norm(y, g, b):
    mu = jnp.mean(y, axis=-1, keepdims=True)
    d = y - mu
    var = jnp.mean(d * d, axis=-1, keepdims=True)
    return d * lax.rsqrt(var + LN_EPS) * g + b


def _head_norm(o, avg):
    mu = _split_dot(o, avg)
    d = o - mu
    var = _split_dot(d * d, avg)
    return d * lax.rsqrt(var + LN_EPS)


def _params(sem):
    return pltpu.CompilerParams(dimension_semantics=sem, vmem_limit_bytes=VMEM_LIMIT)


def _const_spec(shape):
    nd = len(shape)
    return pl.BlockSpec(shape, lambda *_: (0,) * nd, pipeline_mode=pl.Buffered(1))


def _inproj_kernel(x_ref, w_ref, ret_ref, att_ref, gla_ref, s5_ref):
    xb = x_ref[...].astype(BF16)
    ret_ref[...] = _dot(xb, w_ref[:, 0:RET_W]).astype(BF16)
    att_ref[...] = _dot(xb, w_ref[:, RET_W:RET_W + ATT_W]).astype(BF16)
    gla_ref[...] = _dot(xb, w_ref[:, RET_W + ATT_W:RET_W + ATT_W + GLA_W]).astype(BF16)
    s5_ref[...] = _dot(xb, w_ref[:, RET_W + ATT_W + GLA_W:])


def _inproj(x, w_cat, tm):
    bsz, seq, _ = x.shape
    wtot = w_cat.shape[1]
    return pl.pallas_call(
        _inproj_kernel,
        grid=(bsz, seq // tm),
        in_specs=[
            pl.BlockSpec((None, tm, D_MODEL), lambda b, t: (b, t, 0)),
            _const_spec((D_MODEL, wtot)),
        ],
        out_specs=[
            pl.BlockSpec((None, tm, RET_W), lambda b, t: (b, t, 0)),
            pl.BlockSpec((None, tm, ATT_W), lambda b, t: (b, t, 0)),
            pl.BlockSpec((None, tm, GLA_W), lambda b, t: (b, t, 0)),
            pl.BlockSpec((tm, S5_W), lambda b, t: (t, b)),
        ],
        out_shape=[
            jax.ShapeDtypeStruct((bsz, seq, RET_W), BF16),
            jax.ShapeDtypeStruct((bsz, seq, ATT_W), BF16),
            jax.ShapeDtypeStruct((bsz, seq, GLA_W), BF16),
            jax.ShapeDtypeStruct((seq, bsz * S5_W), F32),
        ],
        compiler_params=_params(("parallel", "parallel")),
        name="inproj",
    )(x, w_cat)


def _ret_kernel(r_ref, cos_ref, sin_ref, dmask_ref, xi_ref, zeta_ref, gdec_ref, bmask_ref, avg_ref,
                o_ref, state_ref):
    seq = r_ref.shape[0]
    lane_qk = lax.broadcasted_iota(jnp.int32, (1, QK_W), 1) // RET_DK
    lane_v = lax.broadcasted_iota(jnp.int32, (1, BRANCH_WIDTH), 1) // RET_DV
    state_ref[...] = jnp.zeros_like(state_ref)

    def block(i, carry):
        r0 = pl.multiple_of(i * BLK, BLK)
        rows = pl.ds(r0, BLK)
        cos = cos_ref[rows, :]
        sin = sin_ref[rows, :]
        q = r_ref[rows, 0:128].astype(F32) * cos + r_ref[rows, 128:256].astype(F32) * sin
        k = (r_ref[rows, 256:384].astype(F32) * cos + r_ref[rows, 384:512].astype(F32) * sin) * (RET_DK ** -0.5)
        v = r_ref[rows, 512:768]
        g = r_ref[rows, 768:1024].astype(F32)
        kbd = jnp.concatenate(
            [jnp.where(lane_qk == h, k, 0.0).astype(BF16) for h in range(RET_HEADS)], axis=0)
        vbd = jnp.concatenate(
            [jnp.where(lane_v == h, v, jnp.zeros_like(v)) for h in range(RET_HEADS)], axis=0)
        s = _dot_nt(q.astype(BF16), kbd) * dmask_ref[...]
        state = state_ref[...]
        o = _dot(s.astype(BF16), vbd) + _dot((q * xi_ref[...]).astype(BF16), state.astype(BF16))
        kz_t = (k * zeta_ref[...]).T.astype(BF16)
        state_ref[...] = gdec_ref[...] * state + _dot(kz_t, v) * bmask_ref[...]
        o = _head_norm(o, avg_ref[...])
        o_ref[rows, :] = (g * jax.nn.sigmoid(g) * o).astype(BF16)
        return carry

    lax.fori_loop(0, seq // BLK, block, 0)


def _retention(ret_in, tabs):
    bsz, seq, _ = ret_in.shape
    consts = [tabs["ret_dmask"], tabs["ret_xi"], tabs["ret_zeta"], tabs["ret_gdec"], tabs["ret_bmask"],
              tabs["avg"]]
    return pl.pallas_call(
        _ret_kernel,
        grid=(bsz,),
        in_specs=[pl.BlockSpec((None, seq, RET_W), lambda b: (b, 0, 0)),
                  _const_spec((seq, QK_W)), _const_spec((seq, QK_W))]
                 + [_const_spec(c.shape) for c in consts],
        out_specs=pl.BlockSpec((None, seq, BRANCH_WIDTH), lambda b: (b, 0, 0)),
        out_shape=jax.ShapeDtypeStruct((bsz, seq, BRANCH_WIDTH), BF16),
        scratch_shapes=[pltpu.VMEM((QK_W, BRANCH_WIDTH), F32)],
        compiler_params=_params(("parallel",)),
        name="retention",
    )(ret_in, tabs["rot_cos"], tabs["rot_sin"], *consts)


def _att_kernel(a_ref, bias_ref, o_ref, k_ref, v_ref):
    seq = a_ref.shape[0]
    lane_h = lax.broadcasted_iota(jnp.int32, (1, BRANCH_WIDTH), 1) // ATT_DH
    band_pos = lax.broadcasted_iota(jnp.int32, (1, ATT_BAND), 1)
    k_ref[0:ATT_LEFT, :] = jnp.zeros((ATT_LEFT, BRANCH_WIDTH), BF16)
    v_ref[0:ATT_LEFT, :] = jnp.zeros((ATT_LEFT, BRANCH_WIDTH), BF16)
    k_ref[ATT_LEFT:, :] = a_ref[:, 256:512]
    v_ref[ATT_LEFT:, :] = a_ref[:, 512:768]

    def chunk(i, carry):
        r0 = pl.multiple_of(i * CHUNK, CHUNK)
        q = a_ref[pl.ds(r0, CHUNK), 0:256] * jnp.asarray(ATT_DH ** -0.5, BF16)
        qst = jnp.concatenate(
            [jnp.where(lane_h == h, q, jnp.zeros_like(q)) for h in range(ATT_HEADS)], axis=0)
        kb = k_ref[pl.ds(r0, ATT_BAND), :]
        vb = v_ref[pl.ds(r0, ATT_BAND), :]
        s = _dot_nt(qst, kb) + bias_ref[...]
        s = jnp.where(band_pos >= ATT_LEFT - r0, s, -1e30)
        m = jnp.max(s, axis=-1, keepdims=True)
        p = jnp.exp(s - m)
        l = jnp.sum(p, axis=-1, keepdims=True)
        pv = _dot(p.astype(BF16), vb) / l
        o = jnp.where(lane_h == 0, pv[0:CHUNK], 0.0)
        for h in range(1, ATT_HEADS):
            o = o + jnp.where(lane_h == h, pv[h * CHUNK:(h + 1) * CHUNK], 0.0)
        o_ref[pl.ds(r0, CHUNK), :] = o.astype(BF16)
        return carry

    lax.fori_loop(0, seq // CHUNK, chunk, 0)


def _attention(att_in, bias):
    bsz, seq, _ = att_in.shape
    return pl.pallas_call(
        _att_kernel,
        grid=(bsz,),
        in_specs=[pl.BlockSpec((None, seq, ATT_W), lambda b: (b, 0, 0)), _const_spec(bias.shape)],
        out_specs=pl.BlockSpec((None, seq, BRANCH_WIDTH), lambda b: (b, 0, 0)),
        out_shape=jax.ShapeDtypeStruct((bsz, seq, BRANCH_WIDTH), BF16),
        scratch_shapes=[pltpu.VMEM((ATT_LEFT + seq, BRANCH_WIDTH), BF16),
                        pltpu.VMEM((ATT_LEFT + seq, BRANCH_WIDTH), BF16)],
        compiler_params=_params(("parallel",)),
        name="attention",
    )(att_in, bias)


def _gla_kernel(g_ref, wa_ref, ba_ref, ltri_ref, bones_ref, esel_ref, bmask_ref, avg_ref,
                o_ref, state_ref, ob_ref):
    seq = g_ref.shape[0]
    nchunk = BLK // CHUNK
    tok_chunk = lax.broadcasted_iota(jnp.int32, (1, BLK), 1) // CHUNK
    state_ref[...] = jnp.zeros_like(state_ref)

    def block(i, carry):
        r0 = pl.multiple_of(i * BLK, BLK)
        rows = pl.ds(r0, BLK)
        q = g_ref[rows, 0:128]
        k = g_ref[rows, 128:256].astype(F32)
        v = g_ref[rows, 256:512]
        r = g_ref[rows, 512:768].astype(F32)
        z = _dot(g_ref[rows, 768:896], wa_ref[...]) + ba_ref[...]
        log_a = (jnp.minimum(z, 0.0) - jnp.log(1.0 + jnp.exp(-jnp.abs(z)))) * (1.0 / GLA_GATE_TAU)
        cum = _split_dot_left(ltri_ref[...], log_a)
        tot = _split_dot_left(bones_ref[...], log_a)
        k_dec = k * jnp.exp(tot - cum)
        kd_t = k_dec.T.astype(BF16)
        kd_st = jnp.concatenate(
            [jnp.where(tok_chunk == c, kd_t, jnp.zeros_like(kd_t)) for c in range(nchunk)], axis=0)
        upd = _dot(kd_st, v)
        gch = jnp.exp(_split_dot(log_a.T, esel_ref[...]))
        state = state_ref[...]
        for c in range(nchunk):
            state = gch[:, c * BRANCH_WIDTH:(c + 1) * BRANCH_WIDTH] * state \
                + upd[c * QK_W:(c + 1) * QK_W] * bmask_ref[...]
            ob_ref[c * CHUNK:(c + 1) * CHUNK, :] = _dot(q[c * CHUNK:(c + 1) * CHUNK], state.astype(BF16))
        state_ref[...] = state
        o = _head_norm(ob_ref[...] * (GLA_DK ** -0.5), avg_ref[...])
        o_ref[rows, :] = (r * jax.nn.sigmoid(r) * o).astype(BF16)
        return carry

    lax.fori_loop(0, seq // BLK, block, 0)


def _split_dot_left(a, b):
    hi = b.astype(BF16)
    lo = (b - hi.astype(F32)).astype(BF16)
    return _dot(a, hi) + _dot(a, lo)


def _gla(gla_in, wa_pad, ba, tabs):
    bsz, seq, _ = gla_in.shape
    consts = [wa_pad, ba, tabs["gla_ltri"], tabs["gla_bones"], tabs["gla_esel"], tabs["ret_bmask"], tabs["avg"]]
    return pl.pallas_call(
        _gla_kernel,
        grid=(bsz,),
        in_specs=[pl.BlockSpec((None, seq, GLA_W), lambda b: (b, 0, 0))] + [_const_spec(c.shape) for c in consts],
        out_specs=pl.BlockSpec((None, seq, BRANCH_WIDTH), lambda b: (b, 0, 0)),
        out_shape=jax.ShapeDtypeStruct((bsz, seq, BRANCH_WIDTH), BF16),
        scratch_shapes=[pltpu.VMEM((QK_W, BRANCH_WIDTH), F32), pltpu.VMEM((BLK, BRANCH_WIDTH), F32)],
        compiler_params=_params(("parallel",)),
        name="gla",
    )(gla_in, *consts)


def _s5_kernel(u_ref, bblk_ref, a_ref, cblk_ref, d_ref, wglu_ref, bglu_ref, o_ref, xs_ref, st_ref):
    tlen = u_ref.shape[0]
    half = S5_NS // 2

    @pl.when(pl.program_id(1) == 0)
    def _():
        st_ref[...] = jnp.zeros_like(st_ref)

    u2 = u_ref[...].reshape(tlen * S5_NB, S5_W)
    xs_ref[...] = _dot(u2.astype(BF16), bblk_ref[...])
    for p in range(2):
        re = slice(p * half, (p + 1) * half)
        im = slice(S5_NS + p * half, S5_NS + (p + 1) * half)
        ar = jnp.broadcast_to(a_ref[:, re], (S5_NB, half))
        ai = jnp.broadcast_to(a_ref[:, im], (S5_NB, half))

        def step(t, carry):
            xr, xi = carry
            rows = pl.ds(pl.multiple_of(t * S5_NB, S5_NB), S5_NB)
            nr = ar * xr - ai * xi + xs_ref[rows, re]
            ni = ar * xi + ai * xr + xs_ref[rows, im]
            xs_ref[rows, re] = nr
            xs_ref[rows, im] = ni
            return nr, ni

        xr, xi = lax.fori_loop(0, tlen, step, (st_ref[:, re], st_ref[:, im]), unroll=4)
        st_ref[:, re] = xr
        st_ref[:, im] = xi
    y = _dot(xs_ref[...].astype(BF16), cblk_ref[...]) + d_ref[...] * u2
    y = jax.nn.gelu(y)
    y = y * jax.nn.sigmoid(_dot(y.astype(BF16), wglu_ref[...]) + bglu_ref[...])
    o_ref[...] = y.reshape(tlen, S5_NB, S5_W)


def _s5(u_tm, bblk, a_vec, cblk, d_skip, w_glu, b_glu, tlen):
    seq, bsz, _ = u_tm.shape
    consts = [bblk, a_vec, cblk, d_skip, w_glu, b_glu]
    return pl.pallas_call(
        _s5_kernel,
        grid=(bsz // S5_NB, seq // tlen),
        in_specs=[pl.BlockSpec((tlen, S5_NB, S5_W), lambda b, t: (t, b, 0))]
                 + [_const_spec(c.shape) for c in consts],
        out_specs=pl.BlockSpec((tlen, S5_NB, S5_W), lambda b, t: (t, b, 0)),
        out_shape=jax.ShapeDtypeStruct((seq, bsz, S5_W), F32),
        scratch_shapes=[pltpu.VMEM((tlen * S5_NB, 2 * S5_NS), F32), pltpu.VMEM((S5_NB, 2 * S5_NS), F32)],
        compiler_params=_params(("parallel", "arbitrary")),
        name="s5",
    )(u_tm, *consts)


def _merge_kernel(x_ref, ret_ref, att_ref, gla_ref, s5_ref, wg_ref, bg_ref, wb_ref, wo_ref, g_ref, b_ref, o_ref):
    x = x_ref[...]
    xb = x.astype(BF16)
    branches = (ret_ref[...], att_ref[...], gla_ref[...], s5_ref[...].astype(BF16))
    mixed = None
    for i, br in enumerate(branches):
        gate = jax.nn.sigmoid(_dot(xb, wg_ref[i]) + bg_ref[i])
        term = gate * _dot(br, wb_ref[i])
        mixed = term if mixed is None else mixed + term
    y = DEEPNORM_ALPHA * x + _dot(mixed.astype(BF16), wo_ref[...])
    o_ref[...] = _layer_norm(y, g_ref[...], b_ref[...])


def _merge(x, o_ret, o_att, o_gla, o_s5_tm, wg, bg, wb, wo, ln_g, ln_b, tm):
    bsz, seq, _ = x.shape
    consts = [wg, bg, wb, wo, ln_g, ln_b]
    tok = lambda w: pl.BlockSpec((None, tm, w), lambda b, t: (b, t, 0))
    return pl.pallas_call(
        _merge_kernel,
        grid=(bsz, seq // tm),
        in_specs=[tok(D_MODEL), tok(BRANCH_WIDTH), tok(BRANCH_WIDTH), tok(BRANCH_WIDTH),
                  pl.BlockSpec((tm, S5_W), lambda b, t: (t, b))]
                 + [_const_spec(c.shape) for c in consts],
        out_specs=tok(D_MODEL),
        out_shape=jax.ShapeDtypeStruct((bsz, seq, D_MODEL), F32),
        compiler_params=_params(("parallel", "parallel")),
        name="merge",
    )(x, o_ret, o_att, o_gla, o_s5_tm, *consts)


def _ffn_kernel(x_ref, wg_ref, wu_ref, wd_ref, g_ref, b_ref, o_ref):
    x = x_ref[...]
    xb = x.astype(BF16)
    gate = _dot(xb, wg_ref[...])
    act = (gate * jax.nn.sigmoid(gate) * _dot(xb, wu_ref[...])).astype(BF16)
    y = DEEPNORM_ALPHA * x + _dot(act, wd_ref[...])
    o_ref[...] = _layer_norm(y, g_ref[...], b_ref[...])


def _ffn(x, wg, wu, wd, ln_g, ln_b, tm):
    bsz, seq, _ = x.shape
    consts = [wg, wu, wd, ln_g, ln_b]
    tok = pl.BlockSpec((None, tm, D_MODEL), lambda b, t: (b, t, 0))
    return pl.pallas_call(
        _ffn_kernel,
        grid=(bsz, seq // tm),
        in_specs=[tok] + [_const_spec(c.shape) for c in consts],
        out_specs=tok,
        out_shape=jax.ShapeDtypeStruct((bsz, seq, D_MODEL), F32),
        compiler_params=_params(("parallel", "parallel")),
        name="ffn",
    )(x, *consts)


def _tables(seq):
    f32 = F32
    pos = jnp.arange(seq)
    half = RET_DK // 2
    inv = 1.0 / (10000.0 ** (jnp.arange(half, dtype=f32) / half))
    ang = pos.astype(f32)[:, None] * inv[None, :]
    cos, sin = jnp.cos(ang), jnp.sin(ang)
    cos_h = jnp.concatenate([cos, cos], axis=1)
    sin_h = jnp.concatenate([-sin, sin], axis=1)
    tabs = {"rot_cos": jnp.tile(cos_h, (1, RET_HEADS)), "rot_sin": jnp.tile(sin_h, (1, RET_HEADS))}

    log_g = jnp.log(1.0 - 2.0 ** (-5.0 - jnp.arange(RET_HEADS, dtype=f32)))
    n = jnp.arange(BLK)
    dist = (n[:, None] - n[None, :]).astype(f32)
    cn, cm = n[:, None] // CHUNK, n[None, :] // CHUNK
    expo = jnp.where(cn == cm, jnp.abs(dist), dist)
    dec = jnp.where((cm <= cn)[None], jnp.exp(log_g[:, None, None] * expo[None]), 0.0)
    tabs["ret_dmask"] = jnp.moveaxis(dec, 0, 1).reshape(BLK, RET_HEADS * BLK)
    lane_log_g = jnp.repeat(log_g, RET_DK)[None, :]
    nf = n.astype(f32)[:, None]
    tabs["ret_xi"] = jnp.exp(lane_log_g * (nf + 1.0))
    tabs["ret_zeta"] = jnp.exp(lane_log_g * (BLK - 1.0 - nf))
    bmask = (jnp.arange(QK_W)[:, None] // RET_DK == jnp.arange(BRANCH_WIDTH)[None, :] // RET_DV).astype(f32)
    tabs["ret_bmask"] = bmask
    tabs["ret_gdec"] = jnp.exp(lane_log_g * BLK).reshape(QK_W, 1) * bmask
    hv = jnp.arange(BRANCH_WIDTH) // RET_DV
    tabs["avg"] = ((hv[:, None] == hv[None, :]).astype(f32) / RET_DV).astype(BF16)

    same = cn == cm
    tabs["gla_ltri"] = (same & (n[None, :] <= n[:, None])).astype(BF16)
    tabs["gla_bones"] = same.astype(BF16)
    ecol = jnp.arange((BLK // CHUNK) * BRANCH_WIDTH)[None, :] // BRANCH_WIDTH
    tabs["gla_esel"] = (n[:, None] // CHUNK == ecol).astype(BF16)
    return tabs


def _att_bias(rel_bias):
    jq = jnp.arange(CHUNK)[:, None]
    pk = jnp.arange(ATT_BAND)[None, :]
    rel = jnp.clip(ATT_LEFT + jq - pk, -ATT_MAX_REL, ATT_MAX_REL) + ATT_MAX_REL
    return rel_bias.astype(F32)[:, rel].reshape(ATT_HEADS * CHUNK, ATT_BAND)


def _inproj_weights(w):
    o = _IN_OFF
    col = lambda i, width: w[:, o[i]:o[i] + width]
    j = jnp.arange(QK_W)
    swap = (j // RET_DK) * RET_DK + (j % RET_DK + RET_DK // 2) % RET_DK
    rq, rk = col(0, 128), col(1, 128)
    gate_pad = jnp.zeros((D_MODEL, LANE - GLA_GATE_RANK), w.dtype)
    slabs = [rq, rq[:, swap], rk, rk[:, swap], col(2, 256), col(3, 256),
             col(4, 256), col(5, 256), col(6, 256),
             col(7, 128), col(8, 128), col(9, 256), col(10, 256), col(11, GLA_GATE_RANK), gate_pad,
             col(12, 256)]
    return jnp.concatenate(slabs, axis=1).astype(BF16)


def _s5_matrices(lam_re, lam_im, log_dt, b_re, b_im, c_re, c_im):
    f32 = F32
    lr, li = lam_re.astype(f32), lam_im.astype(f32)
    dt = jnp.exp(log_dt.astype(f32))[:, None]
    mag = jnp.exp(lr * dt)
    ab_re, ab_im = mag * jnp.cos(li * dt), mag * jnp.sin(li * dt)
    den = lr * lr + li * li
    nr, ni = ab_re - 1.0, ab_im
    coef_re = (nr * lr + ni * li) / den
    coef_im = (ni * lr - nr * li) / den
    br, bi = b_re.astype(f32), b_im.astype(f32)
    bb_re = coef_re[..., None] * br - coef_im[..., None] * bi
    bb_im = coef_re[..., None] * bi + coef_im[..., None] * br
    eye = jnp.eye(S5_GROUPS, dtype=f32)

    def in_blockdiag(m):
        return jnp.einsum('gpi,gh->gihp', m, eye).reshape(S5_W, S5_NS)

    def out_blockdiag(m):
        return jnp.einsum('gip,gh->gphi', m, eye).reshape(S5_NS, S5_W)

    bblk = jnp.concatenate([in_blockdiag(bb_re), in_blockdiag(bb_im)], axis=1).astype(BF16)
    cblk = jnp.concatenate([out_blockdiag(c_re.astype(f32)), -out_blockdiag(c_im.astype(f32))], axis=0).astype(BF16)
    a_vec = jnp.concatenate([ab_re.reshape(1, S5_NS), ab_im.reshape(1, S5_NS)], axis=1)
    return bblk, a_vec, cblk


def kernel(x, w_in, gla_w_a, gla_b_a, att_rel_bias, s5_lambda_re, s5_lambda_im, s5_log_dt, s5_b_re, s5_b_im, s5_c_re, s5_c_im, s5_d, s5_w_glu, s5_b_glu, w_gate, b_gate, w_branch, w_out, ln1_g, ln1_b, w_ffn_gate, w_ffn_up, w_ffn_down, ln2_g, ln2_b):
    bsz, seq, _ = x.shape
    assert seq % BLK == 0 and bsz % S5_NB == 0
    tm = min(512, seq)
    s5_tlen = min(128, seq)
    tabs = _tables(seq)
    row = lambda v: v.astype(F32).reshape(1, -1)
    for l in range(DEPTH):
        w_cat = _inproj_weights(w_in[l])
        ret_in, att_in, gla_in, s5_in = _inproj(x, w_cat, tm)
        o_ret = _retention(ret_in, tabs)
        o_att = _attention(att_in, _att_bias(att_rel_bias[l]))
        wa_pad = jnp.concatenate(
            [gla_w_a[l], jnp.zeros((LANE - GLA_GATE_RANK, QK_W), gla_w_a.dtype)], axis=0).astype(BF16)
        o_gla = _gla(gla_in, wa_pad, row(gla_b_a[l]), tabs)
        bblk, a_vec, cblk = _s5_matrices(s5_lambda_re[l], s5_lambda_im[l], s5_log_dt[l], s5_b_re[l], s5_b_im[l],
                                         s5_c_re[l], s5_c_im[l])
        o_s5 = _s5(s5_in.reshape(seq, bsz, S5_W), bblk, a_vec, cblk, row(s5_d[l]),
                   s5_w_glu[l].astype(BF16), row(s5_b_glu[l]), s5_tlen)
        x = _merge(x, o_ret, o_att, o_gla, o_s5.reshape(seq, bsz * S5_W),
                   w_gate[l].astype(BF16), b_gate[l].astype(F32).reshape(N_BRANCH, 1, D_MODEL),
                   w_branch[l].astype(BF16), w_out[l].astype(BF16), row(ln1_g[l]), row(ln1_b[l]), tm)
        x = _ffn(x, w_ffn_gate[l].astype(BF16), w_ffn_up[l].astype(BF16), w_ffn_down[l].astype(BF16),
                 row(ln2_g[l]), row(ln2_b[l]), tm)
    return x
```

```python
import functools
import math

import jax
import jax.numpy as jnp
from jax import lax
from jax.experimental import pallas as pl
from jax.experimental.pallas import tpu as pltpu

F32 = jnp.float32
BF16 = jnp.bfloat16

D_MODEL = 1024
DEPTH = 4
CHUNK = 64
N_BRANCH = 4
BRANCH_WIDTH = D_MODEL // N_BRANCH
RET_HEADS = 4
RET_DV = BRANCH_WIDTH // RET_HEADS
RET_DK = RET_DV // 2
ATT_HEADS = 4
ATT_DH = BRANCH_WIDTH // ATT_HEADS
ATT_LEFT_CHUNKS = 8
ATT_MAX_REL = 128
GLA_HEADS = 4
GLA_DV = BRANCH_WIDTH // GLA_HEADS
GLA_DK = GLA_DV // 2
GLA_GATE_RANK = 16
GLA_GATE_TAU = 16.0
S5_GROUP = 16
S5_GROUPS = BRANCH_WIDTH // S5_GROUP
S5_STATE = 64
D_FF = ((8 * D_MODEL // 3 + 255) // 256) * 256
DEEPNORM_ALPHA = (2.0 * DEPTH) ** 0.25
LN_EPS = 1e-5

_IN_SIZES = (128, 128, 256, 256, 256, 256, 256, 128, 128, 256, 256, GLA_GATE_RANK, 256)
_IN_OFF = tuple(sum(_IN_SIZES[:i]) for i in range(len(_IN_SIZES)))

RET_W = 1024
ATT_W = 768
GLA_W = 896
S5_W = 256
LANE = 128
QK_W = RET_HEADS * RET_DK

BLK = 4 * CHUNK
ATT_LEFT = ATT_LEFT_CHUNKS * CHUNK
ATT_BAND = ATT_LEFT + CHUNK
ATT_GROUP = 4
S5_NB = 8
S5_NS = S5_GROUPS * S5_STATE
VMEM_LIMIT = 56 * 1024 * 1024
SUB_ROWS = 256


def _dot(a, b):
    return jnp.dot(a, b, preferred_element_type=F32)


def _dot_nt(a, b):
    return lax.dot_general(a, b, (((1,), (1,)), ((), ())), preferred_element_type=F32)


def _split_dot(a, b):
    hi = a.astype(BF16)
    lo = (a - hi.astype(F32)).astype(BF16)
    return _dot(hi, b) + _dot(lo, b)


def _layer_norm(y, g, b):
    mu = jnp.mean(y, axis=-1, keepdims=True)
    d = y - mu
    var = jnp.mean(d * d, axis=-1, keepdims=True)
    return d * lax.rsqrt(var + LN_EPS) * g + b


def _head_norm(o, avg):
    mu = _dot(o.astype(BF16), avg)
    d = o - mu
    var = _dot((d * d).astype(BF16), avg)
    return d * lax.rsqrt(var + LN_EPS)


def _params(sem):
    return pltpu.CompilerParams(dimension_semantics=sem, vmem_limit_bytes=VMEM_LIMIT)


def _const_spec(shape):
    nd = len(shape)
    return pl.BlockSpec(shape, lambda *_: (0,) * nd, pipeline_mode=pl.Buffered(1))


def _inproj_kernel(x_ref, w_ref, ret_ref, att_ref, gla_ref, s5_ref):
    for r in range(0, x_ref.shape[0], SUB_ROWS):
        rows = slice(r, r + SUB_ROWS)
        xb = x_ref[rows, :].astype(BF16)
        ret_ref[rows, :] = _dot(xb, w_ref[:, 0:RET_W]).astype(BF16)
        att_ref[rows, :] = _dot(xb, w_ref[:, RET_W:RET_W + ATT_W]).astype(BF16)
        gla_ref[rows, :] = _dot(xb, w_ref[:, RET_W + ATT_W:RET_W + ATT_W + GLA_W]).astype(BF16)
        s5_ref[rows, :] = _dot(xb, w_ref[:, RET_W + ATT_W + GLA_W:])


def _inproj(x, w_cat, tm):
    bsz, seq, _ = x.shape
    wtot = w_cat.shape[1]
    return pl.pallas_call(
        _inproj_kernel,
        grid=(bsz, seq // tm),
        in_specs=[
            pl.BlockSpec((None, tm, D_MODEL), lambda b, t: (b, t, 0)),
            _const_spec((D_MODEL, wtot)),
        ],
        out_specs=[
            pl.BlockSpec((None, tm, RET_W), lambda b, t: (b, t, 0)),
            pl.BlockSpec((None, tm, ATT_W), lambda b, t: (b, t, 0)),
            pl.BlockSpec((None, tm, GLA_W), lambda b, t: (b, t, 0)),
            pl.BlockSpec((tm, S5_W), lambda b, t: (t, b)),
        ],
        out_shape=[
            jax.ShapeDtypeStruct((bsz, seq, RET_W), BF16),
            jax.ShapeDtypeStruct((bsz, seq, ATT_W), BF16),
            jax.ShapeDtypeStruct((bsz, seq, GLA_W), BF16),
            jax.ShapeDtypeStruct((seq, bsz * S5_W), F32),
        ],
        compiler_params=_params(("parallel", "parallel")),
        name="inproj",
    )(x, w_cat)


def _ret_kernel(r_ref, cos_ref, sin_ref, dmask_ref, xi_ref, zeta_ref, gdec_ref, bmask_ref, avg_ref,
                o_ref, state_ref):
    seq = r_ref.shape[0]
    lane_qk = lax.broadcasted_iota(jnp.int32, (1, QK_W), 1) // RET_DK
    lane_v = lax.broadcasted_iota(jnp.int32, (1, BRANCH_WIDTH), 1) // RET_DV
    state_ref[...] = jnp.zeros_like(state_ref)

    def block(i, carry):
        r0 = pl.multiple_of(i * BLK, BLK)
        rows = pl.ds(r0, BLK)
        cos = cos_ref[rows, :]
        sin = sin_ref[rows, :]
        q = r_ref[rows, 0:128].astype(F32) * cos + r_ref[rows, 128:256].astype(F32) * sin
        k = (r_ref[rows, 256:384].astype(F32) * cos + r_ref[rows, 384:512].astype(F32) * sin) * (RET_DK ** -0.5)
        v = r_ref[rows, 512:768]
        g = r_ref[rows, 768:1024].astype(F32)
        kbd = jnp.concatenate(
            [jnp.where(lane_qk == h, k, 0.0).astype(BF16) for h in range(RET_HEADS)], axis=0)
        vbd = jnp.concatenate(
            [jnp.where(lane_v == h, v, jnp.zeros_like(v)) for h in range(RET_HEADS)], axis=0)
        s = _dot_nt(q.astype(BF16), kbd) * dmask_ref[...]
        state = state_ref[...]
        o = _dot(s.astype(BF16), vbd) + _dot((q * xi_ref[...]).astype(BF16), state.astype(BF16))
        kz_t = (k * zeta_ref[...]).T.astype(BF16)
        state_ref[...] = gdec_ref[...] * state + _dot(kz_t, v) * bmask_ref[...]
        o = _head_norm(o, avg_ref[...])
        o_ref[rows, :] = (g * jax.nn.sigmoid(g) * o).astype(BF16)
        return carry

    lax.fori_loop(0, seq // BLK, block, 0, unroll=4)


def _retention(ret_in, tabs):
    bsz, seq, _ = ret_in.shape
    consts = [tabs["ret_dmask"], tabs["ret_xi"], tabs["ret_zeta"], tabs["ret_gdec"], tabs["ret_bmask"],
              tabs["avg"]]
    return pl.pallas_call(
        _ret_kernel,
        grid=(bsz,),
        in_specs=[pl.BlockSpec((None, seq, RET_W), lambda b: (b, 0, 0)),
                  _const_spec((seq, QK_W)), _const_spec((seq, QK_W))]
                 + [_const_spec(c.shape) for c in consts],
        out_specs=pl.BlockSpec((None, seq, BRANCH_WIDTH), lambda b: (b, 0, 0)),
        out_shape=jax.ShapeDtypeStruct((bsz, seq, BRANCH_WIDTH), BF16),
        scratch_shapes=[pltpu.VMEM((QK_W, BRANCH_WIDTH), F32)],
        compiler_params=_params(("parallel",)),
        name="retention",
    )(ret_in, tabs["rot_cos"], tabs["rot_sin"], *consts)


def _att_kernel(a_ref, bias_ref, o_ref, k_ref, v_ref):
    seq = a_ref.shape[0]
    lane_h = lax.broadcasted_iota(jnp.int32, (1, BRANCH_WIDTH), 1) // ATT_DH
    band_pos = lax.broadcasted_iota(jnp.int32, (1, ATT_BAND), 1)
    k_ref[0:ATT_LEFT, :] = jnp.zeros((ATT_LEFT, BRANCH_WIDTH), BF16)
    v_ref[0:ATT_LEFT, :] = jnp.zeros((ATT_LEFT, BRANCH_WIDTH), BF16)
    k_ref[ATT_LEFT:, :] = a_ref[:, 256:512]
    v_ref[ATT_LEFT:, :] = a_ref[:, 512:768]

    def chunk(i, padded):
        r0 = pl.multiple_of(i * CHUNK, CHUNK)
        q = a_ref[pl.ds(r0, CHUNK), 0:256] * jnp.asarray(ATT_DH ** -0.5, BF16)
        qst = jnp.concatenate(
            [jnp.where(lane_h == h, q, jnp.zeros_like(q)) for h in range(ATT_HEADS)], axis=0)
        kb = k_ref[pl.ds(r0, ATT_BAND), :]
        vb = v_ref[pl.ds(r0, ATT_BAND), :]
        s = _dot_nt(qst, kb) + bias_ref[...]
        if padded:
            s = jnp.where(band_pos >= ATT_LEFT - r0, s, -1e30)
        m = jnp.max(s, axis=-1, keepdims=True)
        p = jnp.exp(s - m)
        l = jnp.sum(p, axis=-1, keepdims=True)
        pv = _dot(p.astype(BF16), vb) * (1.0 / l)
        o = jnp.where(lane_h == 0, pv[0:CHUNK], 0.0)
        for h in range(1, ATT_HEADS):
            o = o + jnp.where(lane_h == h, pv[h * CHUNK:(h + 1) * CHUNK], 0.0)
        o_ref[pl.ds(r0, CHUNK), :] = o.astype(BF16)

    nchunk = seq // CHUNK
    npad = min(ATT_LEFT_CHUNKS, nchunk)

    def group(first, padded):
        def body(j, carry):
            for c in range(ATT_GROUP):
                chunk(first + ATT_GROUP * j + c, padded)
            return carry
        return body

    lax.fori_loop(0, npad // ATT_GROUP, group(0, True), 0)
    lax.fori_loop(0, (nchunk - npad) // ATT_GROUP, group(npad, False), 0)


def _attention(att_in, bias):
    bsz, seq, _ = att_in.shape
    return pl.pallas_call(
        _att_kernel,
        grid=(bsz,),
        in_specs=[pl.BlockSpec((None, seq, ATT_W), lambda b: (b, 0, 0)), _const_spec(bias.shape)],
        out_specs=pl.BlockSpec((None, seq, BRANCH_WIDTH), lambda b: (b, 0, 0)),
        out_shape=jax.ShapeDtypeStruct((bsz, seq, BRANCH_WIDTH), BF16),
        scratch_shapes=[pltpu.VMEM((ATT_LEFT + seq, BRANCH_WIDTH), BF16),
                        pltpu.VMEM((ATT_LEFT + seq, BRANCH_WIDTH), BF16)],
        compiler_params=_params(("parallel",)),
        name="attention",
    )(att_in, bias)


def _gla_kernel(g_ref, wa_ref, ba_ref, ltri_ref, bones_ref, esel_ref, bmask_ref, avg_ref,
                o_ref, state_ref, ob_ref):
    seq = g_ref.shape[0]
    nchunk = BLK // CHUNK
    tok_chunk = lax.broadcasted_iota(jnp.int32, (1, BLK), 1) // CHUNK
    state_ref[...] = jnp.zeros_like(state_ref)

    def block(i, carry):
        r0 = pl.multiple_of(i * BLK, BLK)
        rows = pl.ds(r0, BLK)
        q = g_ref[rows, 0:128]
        k = g_ref[rows, 128:256].astype(F32)
        v = g_ref[rows, 256:512]
        r = g_ref[rows, 512:768].astype(F32)
        z = _dot(g_ref[rows, 768:896], wa_ref[...]) + ba_ref[...]
        log_a = (jnp.minimum(z, 0.0) - jnp.log(1.0 + jnp.exp(-jnp.abs(z)))) * (1.0 / GLA_GATE_TAU)
        cum = _split_dot_left(ltri_ref[...], log_a)
        tot = _split_dot_left(bones_ref[...], log_a)
        k_dec = k * jnp.exp(tot - cum)
        kd_t = k_dec.T.astype(BF16)
        kd_st = jnp.concatenate(
            [jnp.where(tok_chunk == c, kd_t, jnp.zeros_like(kd_t)) for c in range(nchunk)], axis=0)
        upd = _dot(kd_st, v)
        gch = jnp.exp(_split_dot(log_a.T, esel_ref[...]))
        state = state_ref[...]
        for c in range(nchunk):
            state = gch[:, c * BRANCH_WIDTH:(c + 1) * BRANCH_WIDTH] * state \
                + upd[c * QK_W:(c + 1) * QK_W] * bmask_ref[...]
            ob_ref[c * CHUNK:(c + 1) * CHUNK, :] = _dot(q[c * CHUNK:(c + 1) * CHUNK], state.astype(BF16))
        state_ref[...] = state
        o = _head_norm(ob_ref[...] * (GLA_DK ** -0.5), avg_ref[...])
        o_ref[rows, :] = (r * jax.nn.sigmoid(r) * o).astype(BF16)
        return carry

    lax.fori_loop(0, seq // BLK, block, 0, unroll=4)


def _split_dot_left(a, b):
    hi = b.astype(BF16)
    lo = (b - hi.astype(F32)).astype(BF16)
    return _dot(a, hi) + _dot(a, lo)


def _gla(gla_in, wa_pad, ba, tabs):
    bsz, seq, _ = gla_in.shape
    consts = [wa_pad, ba, tabs["gla_ltri"], tabs["gla_bones"], tabs["gla_esel"], tabs["ret_bmask"], tabs["avg"]]
    return pl.pallas_call(
        _gla_kernel,
        grid=(bsz,),
        in_specs=[pl.BlockSpec((None, seq, GLA_W), lambda b: (b, 0, 0))] + [_const_spec(c.shape) for c in consts],
        out_specs=pl.BlockSpec((None, seq, BRANCH_WIDTH), lambda b: (b, 0, 0)),
        out_shape=jax.ShapeDtypeStruct((bsz, seq, BRANCH_WIDTH), BF16),
        scratch_shapes=[pltpu.VMEM((QK_W, BRANCH_WIDTH), F32), pltpu.VMEM((BLK, BRANCH_WIDTH), F32)],
        compiler_params=_params(("parallel",)),
        name="gla",
    )(gla_in, *consts)


def _s5_kernel(u_ref, bblk_ref, a_ref, cblk_ref, d_ref, wglu_ref, bglu_ref, o_ref, xs_ref, st_ref):
    tlen = u_ref.shape[0]
    half = S5_NS // 2

    @pl.when(pl.program_id(1) == 0)
    def _():
        st_ref[...] = jnp.zeros_like(st_ref)

    u2 = u_ref[...].reshape(tlen * S5_NB, S5_W)
    xs_ref[...] = _dot(u2.astype(BF16), bblk_ref[...])
    for p in range(2):
        re = slice(p * half, (p + 1) * half)
        im = slice(S5_NS + p * half, S5_NS + (p + 1) * half)
        ar = jnp.broadcast_to(a_ref[:, re], (S5_NB, half))
        ai = jnp.broadcast_to(a_ref[:, im], (S5_NB, half))

        def step(t, carry):
            xr, xi = carry
            rows = pl.ds(pl.multiple_of(t * S5_NB, S5_NB), S5_NB)
            nr = ar * xr - ai * xi + xs_ref[rows, re]
            ni = ar * xi + ai * xr + xs_ref[rows, im]
            xs_ref[rows, re] = nr
            xs_ref[rows, im] = ni
            return nr, ni

        xr, xi = lax.fori_loop(0, tlen, step, (st_ref[:, re], st_ref[:, im]), unroll=4)
        st_ref[:, re] = xr
        st_ref[:, im] = xi
    for r in range(0, tlen * S5_NB, SUB_ROWS):
        rows = slice(r, r + SUB_ROWS)
        steps = slice(r // S5_NB, (r + SUB_ROWS) // S5_NB)
        u_sub = u_ref[steps].reshape(SUB_ROWS, S5_W)
        y = _dot(xs_ref[rows, :].astype(BF16), cblk_ref[...]) + d_ref[...] * u_sub
        y = jax.nn.gelu(y)
        y = y * jax.nn.sigmoid(_dot(y.astype(BF16), wglu_ref[...]) + bglu_ref[...])
        o_ref[steps] = y.reshape(SUB_ROWS // S5_NB, S5_NB, S5_W)


def _s5(u_tm, bblk, a_vec, cblk, d_skip, w_glu, b_glu, tlen):
    seq, bsz, _ = u_tm.shape
    consts = [bblk, a_vec, cblk, d_skip, w_glu, b_glu]
    return pl.pallas_call(
        _s5_kernel,
        grid=(bsz // S5_NB, seq // tlen),
        in_specs=[pl.BlockSpec((tlen, S5_NB, S5_W), lambda b, t: (t, b, 0))]
                 + [_const_spec(c.shape) for c in consts],
        out_specs=pl.BlockSpec((tlen, S5_NB, S5_W), lambda b, t: (t, b, 0)),
        out_shape=jax.ShapeDtypeStruct((seq, bsz, S5_W), F32),
        scratch_shapes=[pltpu.VMEM((tlen * S5_NB, 2 * S5_NS), F32), pltpu.VMEM((S5_NB, 2 * S5_NS), F32)],
        compiler_params=_params(("parallel", "arbitrary")),
        name="s5",
    )(u_tm, *consts)


def _merge_kernel(x_ref, ret_ref, att_ref, gla_ref, s5_ref, wg_ref, bg_ref, wb_ref, wo_ref, g_ref, b_ref, o_ref):
    for r in range(0, x_ref.shape[0], SUB_ROWS):
        rows = slice(r, r + SUB_ROWS)
        x = x_ref[rows, :]
        xb = x.astype(BF16)
        branches = (ret_ref[rows, :], att_ref[rows, :], gla_ref[rows, :], s5_ref[rows, :].astype(BF16))
        mixed = None
        for i, br in enumerate(branches):
            gate = jax.nn.sigmoid(_dot(xb, wg_ref[i]) + bg_ref[i])
            term = gate * _dot(br, wb_ref[i])
            mixed = term if mixed is None else mixed + term
        y = DEEPNORM_ALPHA * x + _dot(mixed.astype(BF16), wo_ref[...])
        o_ref[rows, :] = _layer_norm(y, g_ref[...], b_ref[...])


def _merge(x, o_ret, o_att, o_gla, o_s5_tm, wg, bg, wb, wo, ln_g, ln_b, tm):
    bsz, seq, _ = x.shape
    consts = [wg, bg, wb, wo, ln_g, ln_b]
    tok = lambda w: pl.BlockSpec((None, tm, w), lambda b, t: (b, t, 0))
    return pl.pallas_call(
        _merge_kernel,
        grid=(bsz, seq // tm),
        in_specs=[tok(D_MODEL), tok(BRANCH_WIDTH), tok(BRANCH_WIDTH), tok(BRANCH_WIDTH),
                  pl.BlockSpec((tm, S5_W), lambda b, t: (t, b))]
                 + [_const_spec(c.shape) for c in consts],
        out_specs=tok(D_MODEL),
        out_shape=jax.ShapeDtypeStruct((bsz, seq, D_MODEL), F32),
        compiler_params=_params(("parallel", "parallel")),
        name="merge",
    )(x, o_ret, o_att, o_gla, o_s5_tm, *consts)


def _ffn_kernel(x_ref, wg_ref, wu_ref, wd_ref, g_ref, b_ref, o_ref):
    for r in range(0, x_ref.shape[0], SUB_ROWS):
        rows = slice(r, r + SUB_ROWS)
        x = x_ref[rows, :]
        xb = x.astype(BF16)
        gate = _dot(xb, wg_ref[...])
        act = (gate * jax.nn.sigmoid(gate) * _dot(xb, wu_ref[...])).astype(BF16)
        y = DEEPNORM_ALPHA * x + _dot(act, wd_ref[...])
        o_ref[rows, :] = _layer_norm(y, g_ref[...], b_ref[...])


def _ffn(x, wg, wu, wd, ln_g, ln_b, tm):
    bsz, seq, _ = x.shape
    consts = [wg, wu, wd, ln_g, ln_b]
    tok = pl.BlockSpec((None, tm, D_MODEL), lambda b, t: (b, t, 0))
    return pl.pallas_call(
        _ffn_kernel,
        grid=(bsz, seq // tm),
        in_specs=[tok] + [_const_spec(c.shape) for c in consts],
        out_specs=tok,
        out_shape=jax.ShapeDtypeStruct((bsz, seq, D_MODEL), F32),
        compiler_params=_params(("parallel", "parallel")),
        name="ffn",
    )(x, *consts)


def _tables(seq):
    f32 = F32
    pos = jnp.arange(seq)
    half = RET_DK // 2
    inv = 1.0 / (10000.0 ** (jnp.arange(half, dtype=f32) / half))
    ang = pos.astype(f32)[:, None] * inv[None, :]
    cos, sin = jnp.cos(ang), jnp.sin(ang)
    cos_h = jnp.concatenate([cos, cos], axis=1)
    sin_h = jnp.concatenate([-sin, sin], axis=1)
    tabs = {"rot_cos": jnp.tile(cos_h, (1, RET_HEADS)), "rot_sin": jnp.tile(sin_h, (1, RET_HEADS))}

    log_g = jnp.log(1.0 - 2.0 ** (-5.0 - jnp.arange(RET_HEADS, dtype=f32)))
    n = jnp.arange(BLK)
    dist = (n[:, None] - n[None, :]).astype(f32)
    cn, cm = n[:, None] // CHUNK, n[None, :] // CHUNK
    expo = jnp.where(cn == cm, jnp.abs(dist), dist)
    dec = jnp.where((cm <= cn)[None], jnp.exp(log_g[:, None, None] * expo[None]), 0.0)
    tabs["ret_dmask"] = jnp.moveaxis(dec, 0, 1).reshape(BLK, RET_HEADS * BLK)
    lane_log_g = jnp.repeat(log_g, RET_DK)[None, :]
    nf = n.astype(f32)[:, None]
    tabs["ret_xi"] = jnp.exp(lane_log_g * (nf + 1.0))
    tabs["ret_zeta"] = jnp.exp(lane_log_g * (BLK - 1.0 - nf))
    bmask = (jnp.arange(QK_W)[:, None] // RET_DK == jnp.arange(BRANCH_WIDTH)[None, :] // RET_DV).astype(f32)
    tabs["ret_bmask"] = bmask
    tabs["ret_gdec"] = jnp.exp(lane_log_g * BLK).reshape(QK_W, 1) * bmask
    hv = jnp.arange(BRANCH_WIDTH) // RET_DV
    tabs["avg"] = ((hv[:, None] == hv[None, :]).astype(f32) / RET_DV).astype(BF16)

    same = cn == cm
    tabs["gla_ltri"] = (same & (n[None, :] <= n[:, None])).astype(BF16)
    tabs["gla_bones"] = same.astype(BF16)
    ecol = jnp.arange((BLK // CHUNK) * BRANCH_WIDTH)[None, :] // BRANCH_WIDTH
    tabs["gla_esel"] = (n[:, None] // CHUNK == ecol).astype(BF16)
    return tabs


def _att_bias(rel_bias):
    jq = jnp.arange(CHUNK)[:, None]
    pk = jnp.arange(ATT_BAND)[None, :]
    rel = jnp.clip(ATT_LEFT + jq - pk, -ATT_MAX_REL, ATT_MAX_REL) + ATT_MAX_REL
    return rel_bias.astype(F32)[:, rel].reshape(ATT_HEADS * CHUNK, ATT_BAND)


def _inproj_weights(w):
    o = _IN_OFF
    col = lambda i, width: w[:, o[i]:o[i] + width]

    def swap(m):
        return m.reshape(D_MODEL, RET_HEADS, 2, RET_DK // 2)[:, :, ::-1, :].reshape(D_MODEL, QK_W)

    rq, rk = col(0, 128), col(1, 128)
    gate_pad = jnp.zeros((D_MODEL, LANE - GLA_GATE_RANK), w.dtype)
    slabs = [rq, swap(rq), rk, swap(rk), col(2, 256), col(3, 256),
             col(4, 256), col(5, 256), col(6, 256),
             col(7, 128), col(8, 128), col(9, 256), col(10, 256), col(11, GLA_GATE_RANK), gate_pad,
             col(12, 256)]
    return jnp.concatenate(slabs, axis=1).astype(BF16)


def _s5_matrices(lam_re, lam_im, log_dt, b_re, b_im, c_re, c_im):
    f32 = F32
    lr, li = lam_re.astype(f32), lam_im.astype(f32)
    dt = jnp.exp(log_dt.astype(f32))[:, None]
    mag = jnp.exp(lr * dt)
    ab_re, ab_im = mag * jnp.cos(li * dt), mag * jnp.sin(li * dt)
    den = lr * lr + li * li
    nr, ni = ab_re - 1.0, ab_im
    coef_re = (nr * lr + ni * li) / den
    coef_im = (ni * lr - nr * li) / den
    br, bi = b_re.astype(f32), b_im.astype(f32)
    bb_re = coef_re[..., None] * br - coef_im[..., None] * bi
    bb_im = coef_re[..., None] * bi + coef_im[..., None] * br
    eye = jnp.eye(S5_GROUPS, dtype=f32)

    def in_blockdiag(m):
        return jnp.einsum('gpi,gh->gihp', m, eye).reshape(S5_W, S5_NS)

    def out_blockdiag(m):
        return jnp.einsum('gip,gh->gphi', m, eye).reshape(S5_NS, S5_W)

    bblk = jnp.concatenate([in_blockdiag(bb_re), in_blockdiag(bb_im)], axis=1).astype(BF16)
    cblk = jnp.concatenate([out_blockdiag(c_re.astype(f32)), -out_blockdiag(c_im.astype(f32))], axis=0).astype(BF16)
    a_vec = jnp.concatenate([ab_re.reshape(1, S5_NS), ab_im.reshape(1, S5_NS)], axis=1)
    return bblk, a_vec, cblk


def kernel(x, w_in, gla_w_a, gla_b_a, att_rel_bias, s5_lambda_re, s5_lambda_im, s5_log_dt, s5_b_re, s5_b_im, s5_c_re, s5_c_im, s5_d, s5_w_glu, s5_b_glu, w_gate, b_gate, w_branch, w_out, ln1_g, ln1_b, w_ffn_gate, w_ffn_up, w_ffn_down, ln2_g, ln2_b):
    bsz, seq, _ = x.shape
    assert seq % (2 * BLK) == 0 and bsz % S5_NB == 0
    tm = min(1024, seq)
    s5_tlen = min(128, seq)
    tabs = _tables(seq)
    row = lambda v: v.astype(F32).reshape(1, -1)
    for l in range(DEPTH):
        w_cat = _inproj_weights(w_in[l])
        ret_in, att_in, gla_in, s5_in = _inproj(x, w_cat, tm)
        o_ret = _retention(ret_in, tabs)
        o_att = _attention(att_in, _att_bias(att_rel_bias[l]))
        wa_pad = jnp.concatenate(
            [gla_w_a[l], jnp.zeros((LANE - GLA_GATE_RANK, QK_W), gla_w_a.dtype)], axis=0).astype(BF16)
        o_gla = _gla(gla_in, wa_pad, row(gla_b_a[l]), tabs)
        bblk, a_vec, cblk = _s5_matrices(s5_lambda_re[l], s5_lambda_im[l], s5_log_dt[l], s5_b_re[l], s5_b_im[l],
                                         s5_c_re[l], s5_c_im[l])
        o_s5 = _s5(s5_in.reshape(seq, bsz, S5_W), bblk, a_vec, cblk, row(s5_d[l]),
                   s5_w_glu[l].astype(BF16), row(s5_b_glu[l]), s5_tlen)
        x = _merge(x, o_ret, o_att, o_gla, o_s5.reshape(seq, bsz * S5_W),
                   w_gate[l].astype(BF16), b_gate[l].astype(F32).reshape(N_BRANCH, 1, D_MODEL),
                   w_branch[l].astype(BF16), w_out[l].astype(BF16), row(ln1_g[l]), row(ln1_b[l]), tm)
        x = _ffn(x, w_ffn_gate[l].astype(BF16), w_ffn_up[l].astype(BF16), w_ffn_down[l].astype(BF16),
                 row(ln2_g[l]), row(ln2_b[l]), tm)
    return x
```

```python
import functools
import math

import jax
import jax.numpy as jnp
from jax import lax
from jax.experimental import pallas as pl
from jax.experimental.pallas import tpu as pltpu

F32 = jnp.float32
BF16 = jnp.bfloat16

D_MODEL = 1024
DEPTH = 4
CHUNK = 64
N_BRANCH = 4
BRANCH_WIDTH = D_MODEL // N_BRANCH
RET_HEADS = 4
RET_DV = BRANCH_WIDTH // RET_HEADS
RET_DK = RET_DV // 2
ATT_HEADS = 4
ATT_DH = BRANCH_WIDTH // ATT_HEADS
ATT_LEFT_CHUNKS = 8
ATT_MAX_REL = 128
GLA_HEADS = 4
GLA_DV = BRANCH_WIDTH // GLA_HEADS
GLA_DK = GLA_DV // 2
GLA_GATE_RANK = 16
GLA_GATE_TAU = 16.0
S5_GROUP = 16
S5_GROUPS = BRANCH_WIDTH // S5_GROUP
S5_STATE = 64
D_FF = ((8 * D_MODEL // 3 + 255) // 256) * 256
DEEPNORM_ALPHA = (2.0 * DEPTH) ** 0.25
LN_EPS = 1e-5

_IN_SIZES = (128, 128, 256, 256, 256, 256, 256, 128, 128, 256, 256, GLA_GATE_RANK, 256)
_IN_OFF = tuple(sum(_IN_SIZES[:i]) for i in range(len(_IN_SIZES)))

RET_W = 1024
ATT_W = 768
GLA_W = 896
S5_W = 256
LANE = 128
QK_W = RET_HEADS * RET_DK

BLK = 4 * CHUNK
ATT_LEFT = ATT_LEFT_CHUNKS * CHUNK
ATT_BAND = ATT_LEFT + CHUNK
ATT_GROUP = 4
RET_GROUP = 4
S5_NB = 8
S5_NS = S5_GROUPS * S5_STATE
VMEM_LIMIT = 56 * 1024 * 1024
SUB_ROWS = 256


def _dot(a, b):
    return jnp.dot(a, b, preferred_element_type=F32)


def _dot_nt(a, b):
    return lax.dot_general(a, b, (((1,), (1,)), ((), ())), preferred_element_type=F32)


def _split_dot(a, b):
    hi = a.astype(BF16)
    lo = (a - hi.astype(F32)).astype(BF16)
    return _dot(hi, b) + _dot(lo, b)


def _layer_norm(y, g, b):
    mu = jnp.mean(y, axis=-1, keepdims=True)
    d = y - mu
    var = jnp.mean(d * d, axis=-1, keepdims=True)
    return d * lax.rsqrt(var + LN_EPS) * g + b


def _head_norm(o, avg):
    mu = _dot(o.astype(BF16), avg)
    d = o - mu
    var = _dot((d * d).astype(BF16), avg)
    return d * lax.rsqrt(var + LN_EPS)


def _params(sem):
    return pltpu.CompilerParams(dimension_semantics=sem, vmem_limit_bytes=VMEM_LIMIT)


def _const_spec(shape):
    nd = len(shape)
    return pl.BlockSpec(shape, lambda *_: (0,) * nd, pipeline_mode=pl.Buffered(1))


def _inproj_kernel(x_ref, w_ref, ret_ref, att_ref, gla_ref, s5_ref):
    for r in range(0, x_ref.shape[0], SUB_ROWS):
        rows = slice(r, r + SUB_ROWS)
        xb = x_ref[rows, :].astype(BF16)
        ret_ref[rows, :] = _dot(xb, w_ref[:, 0:RET_W]).astype(BF16)
        att_ref[rows, :] = _dot(xb, w_ref[:, RET_W:RET_W + ATT_W]).astype(BF16)
        gla_ref[rows, :] = _dot(xb, w_ref[:, RET_W + ATT_W:RET_W + ATT_W + GLA_W]).astype(BF16)
        s5_ref[rows, :] = _dot(xb, w_ref[:, RET_W + ATT_W + GLA_W:])


def _inproj(x, w_cat, tm):
    bsz, seq, _ = x.shape
    wtot = w_cat.shape[1]
    return pl.pallas_call(
        _inproj_kernel,
        grid=(bsz, seq // tm),
        in_specs=[
            pl.BlockSpec((None, tm, D_MODEL), lambda b, t: (b, t, 0)),
            _const_spec((D_MODEL, wtot)),
        ],
        out_specs=[
            pl.BlockSpec((None, tm, RET_W), lambda b, t: (b, t, 0)),
            pl.BlockSpec((None, tm, ATT_W), lambda b, t: (b, t, 0)),
            pl.BlockSpec((None, tm, GLA_W), lambda b, t: (b, t, 0)),
            pl.BlockSpec((tm, S5_W), lambda b, t: (t, b)),
        ],
        out_shape=[
            jax.ShapeDtypeStruct((bsz, seq, RET_W), BF16),
            jax.ShapeDtypeStruct((bsz, seq, ATT_W), BF16),
            jax.ShapeDtypeStruct((bsz, seq, GLA_W), BF16),
            jax.ShapeDtypeStruct((seq, bsz * S5_W), F32),
        ],
        compiler_params=_params(("parallel", "parallel")),
        name="inproj",
    )(x, w_cat)


def _ret_kernel(r_ref, cos_ref, sin_ref, dmask_ref, xi_ref, zeta_ref, gdec_ref, bmask_ref, avg_ref,
                o_ref, state_ref):
    seq = r_ref.shape[0]
    lane_qk = lax.broadcasted_iota(jnp.int32, (1, QK_W), 1) // RET_DK
    lane_v = lax.broadcasted_iota(jnp.int32, (1, BRANCH_WIDTH), 1) // RET_DV
    state_ref[...] = jnp.zeros_like(state_ref)

    def rotary(rows):
        cos = cos_ref[rows, :]
        sin = sin_ref[rows, :]
        q = r_ref[rows, 0:128].astype(F32) * cos + r_ref[rows, 128:256].astype(F32) * sin
        k = (r_ref[rows, 256:384].astype(F32) * cos + r_ref[rows, 384:512].astype(F32) * sin) * (RET_DK ** -0.5)
        return q, k

    def intra(rows, q, k):
        v = r_ref[rows, 512:768]
        kbd = jnp.concatenate(
            [jnp.where(lane_qk == h, k, 0.0).astype(BF16) for h in range(RET_HEADS)], axis=0)
        vbd = jnp.concatenate(
            [jnp.where(lane_v == h, v, jnp.zeros_like(v)) for h in range(RET_HEADS)], axis=0)
        s = _dot_nt(q.astype(BF16), kbd) * dmask_ref[...]
        return _dot(s.astype(BF16), vbd)

    def update(rows, k):
        kz_t = (k * zeta_ref[...]).T.astype(BF16)
        return _dot(kz_t, r_ref[rows, 512:768]) * bmask_ref[...]

    def finish(rows, o):
        g = r_ref[rows, 768:1024].astype(F32)
        o_ref[rows, :] = (g * jax.nn.sigmoid(g) * _head_norm(o, avg_ref[...])).astype(BF16)

    def group(j, carry):
        rows = [pl.ds(pl.multiple_of((j * RET_GROUP + b) * BLK, BLK), BLK) for b in range(RET_GROUP)]
        qk = [rotary(r) for r in rows]
        o_in = [intra(r, q, k) for r, (q, k) in zip(rows, qk)]
        upd = [update(r, k) for r, (_, k) in zip(rows, qk)]
        state = state_ref[...]
        o_x = []
        for b in range(RET_GROUP):
            o_x.append(_dot((qk[b][0] * xi_ref[...]).astype(BF16), state.astype(BF16)))
            state = gdec_ref[...] * state + upd[b]
        state_ref[...] = state
        for b in range(RET_GROUP):
            finish(rows[b], o_in[b] + o_x[b])
        return carry

    lax.fori_loop(0, seq // (RET_GROUP * BLK), group, 0)


def _retention(ret_in, tabs):
    bsz, seq, _ = ret_in.shape
    consts = [tabs["ret_dmask"], tabs["ret_xi"], tabs["ret_zeta"], tabs["ret_gdec"], tabs["ret_bmask"],
              tabs["avg"]]
    return pl.pallas_call(
        _ret_kernel,
        grid=(bsz,),
        in_specs=[pl.BlockSpec((None, seq, RET_W), lambda b: (b, 0, 0)),
                  _const_spec((seq, QK_W)), _const_spec((seq, QK_W))]
                 + [_const_spec(c.shape) for c in consts],
        out_specs=pl.BlockSpec((None, seq, BRANCH_WIDTH), lambda b: (b, 0, 0)),
        out_shape=jax.ShapeDtypeStruct((bsz, seq, BRANCH_WIDTH), BF16),
        scratch_shapes=[pltpu.VMEM((QK_W, BRANCH_WIDTH), F32)],
        compiler_params=_params(("parallel",)),
        name="retention",
    )(ret_in, tabs["rot_cos"], tabs["rot_sin"], *consts)


def _att_kernel(a_ref, bias_ref, o_ref, k_ref, v_ref):
    seq = a_ref.shape[0]
    lane_h = lax.broadcasted_iota(jnp.int32, (1, BRANCH_WIDTH), 1) // ATT_DH
    band_pos = lax.broadcasted_iota(jnp.int32, (1, ATT_BAND), 1)
    k_ref[0:ATT_LEFT, :] = jnp.zeros((ATT_LEFT, BRANCH_WIDTH), BF16)
    v_ref[0:ATT_LEFT, :] = jnp.zeros((ATT_LEFT, BRANCH_WIDTH), BF16)
    k_ref[ATT_LEFT:, :] = a_ref[:, 256:512]
    v_ref[ATT_LEFT:, :] = a_ref[:, 512:768]

    def scores(r0, padded):
        q = a_ref[pl.ds(r0, CHUNK), 0:256] * jnp.asarray(ATT_DH ** -0.5, BF16)
        qst = jnp.concatenate(
            [jnp.where(lane_h == h, q, jnp.zeros_like(q)) for h in range(ATT_HEADS)], axis=0)
        s = _dot_nt(qst, k_ref[pl.ds(r0, ATT_BAND), :]) + bias_ref[...]
        if padded:
            s = jnp.where(band_pos >= ATT_LEFT - r0, s, -1e30)
        return s

    def softmax(s):
        p = jnp.exp(s - jnp.max(s, axis=-1, keepdims=True))
        return p.astype(BF16), 1.0 / jnp.sum(p, axis=-1, keepdims=True)

    def output(r0, p, inv_l):
        pv = _dot(p, v_ref[pl.ds(r0, ATT_BAND), :]) * inv_l
        o = jnp.where(lane_h == 0, pv[0:CHUNK], 0.0)
        for h in range(1, ATT_HEADS):
            o = o + jnp.where(lane_h == h, pv[h * CHUNK:(h + 1) * CHUNK], 0.0)
        o_ref[pl.ds(r0, CHUNK), :] = o.astype(BF16)

    nchunk = seq // CHUNK
    npad = min(ATT_LEFT_CHUNKS, nchunk)

    def group(first, padded):
        def body(j, carry):
            r0s = [pl.multiple_of((first + ATT_GROUP * j + c) * CHUNK, CHUNK) for c in range(ATT_GROUP)]
            ss = [scores(r0, padded) for r0 in r0s]
            ps = [softmax(s) for s in ss]
            for r0, (p, inv_l) in zip(r0s, ps):
                output(r0, p, inv_l)
            return carry
        return body

    lax.fori_loop(0, npad // ATT_GROUP, group(0, True), 0)
    lax.fori_loop(0, (nchunk - npad) // ATT_GROUP, group(npad, False), 0)


def _attention(att_in, bias):
    bsz, seq, _ = att_in.shape
    return pl.pallas_call(
        _att_kernel,
        grid=(bsz,),
        in_specs=[pl.BlockSpec((None, seq, ATT_W), lambda b: (b, 0, 0)), _const_spec(bias.shape)],
        out_specs=pl.BlockSpec((None, seq, BRANCH_WIDTH), lambda b: (b, 0, 0)),
        out_shape=jax.ShapeDtypeStruct((bsz, seq, BRANCH_WIDTH), BF16),
        scratch_shapes=[pltpu.VMEM((ATT_LEFT + seq, BRANCH_WIDTH), BF16),
                        pltpu.VMEM((ATT_LEFT + seq, BRANCH_WIDTH), BF16)],
        compiler_params=_params(("parallel",)),
        name="attention",
    )(att_in, bias)


def _gla_kernel(g_ref, wa_ref, ba_ref, ltri_ref, bones_ref, esel_ref, bmask_ref, avg_ref,
                o_ref, state_ref, ob_ref):
    seq = g_ref.shape[0]
    nchunk = BLK // CHUNK
    tok_chunk = lax.broadcasted_iota(jnp.int32, (1, BLK), 1) // CHUNK
    state_ref[...] = jnp.zeros_like(state_ref)

    def block(i, carry):
        r0 = pl.multiple_of(i * BLK, BLK)
        rows = pl.ds(r0, BLK)
        q = g_ref[rows, 0:128]
        k = g_ref[rows, 128:256].astype(F32)
        v = g_ref[rows, 256:512]
        r = g_ref[rows, 512:768].astype(F32)
        z = _dot(g_ref[rows, 768:896], wa_ref[...]) + ba_ref[...]
        log_a = (jnp.minimum(z, 0.0) - jnp.log(1.0 + jnp.exp(-jnp.abs(z)))) * (1.0 / GLA_GATE_TAU)
        cum = _split_dot_left(ltri_ref[...], log_a)
        tot = _split_dot_left(bones_ref[...], log_a)
        k_dec = k * jnp.exp(tot - cum)
        kd_t = k_dec.T.astype(BF16)
        kd_st = jnp.concatenate(
            [jnp.where(tok_chunk == c, kd_t, jnp.zeros_like(kd_t)) for c in range(nchunk)], axis=0)
        upd = _dot(kd_st, v)
        gch = jnp.exp(_split_dot(log_a.T, esel_ref[...]))
        state = state_ref[...]
        for c in range(nchunk):
            state = gch[:, c * BRANCH_WIDTH:(c + 1) * BRANCH_WIDTH] * state \
                + upd[c * QK_W:(c + 1) * QK_W] * bmask_ref[...]
            ob_ref[c * CHUNK:(c + 1) * CHUNK, :] = _dot(q[c * CHUNK:(c + 1) * CHUNK], state.astype(BF16))
        state_ref[...] = state
        o = _head_norm(ob_ref[...] * (GLA_DK ** -0.5), avg_ref[...])
        o_ref[rows, :] = (r * jax.nn.sigmoid(r) * o).astype(BF16)
        return carry

    lax.fori_loop(0, seq // BLK, block, 0, unroll=4)


def _split_dot_left(a, b):
    hi = b.astype(BF16)
    lo = (b - hi.astype(F32)).astype(BF16)
    return _dot(a, hi) + _dot(a, lo)


def _gla(gla_in, wa_pad, ba, tabs):
    bsz, seq, _ = gla_in.shape
    consts = [wa_pad, ba, tabs["gla_ltri"], tabs["gla_bones"], tabs["gla_esel"], tabs["ret_bmask"], tabs["avg"]]
    return pl.pallas_call(
        _gla_kernel,
        grid=(bsz,),
        in_specs=[pl.BlockSpec((None, seq, GLA_W), lambda b: (b, 0, 0))] + [_const_spec(c.shape) for c in consts],
        out_specs=pl.BlockSpec((None, seq, BRANCH_WIDTH), lambda b: (b, 0, 0)),
        out_shape=jax.ShapeDtypeStruct((bsz, seq, BRANCH_WIDTH), BF16),
        scratch_shapes=[pltpu.VMEM((QK_W, BRANCH_WIDTH), F32), pltpu.VMEM((BLK, BRANCH_WIDTH), F32)],
        compiler_params=_params(("parallel",)),
        name="gla",
    )(gla_in, *consts)


def _s5_kernel(u_ref, bblk_ref, a_ref, cblk_ref, d_ref, wglu_ref, bglu_ref, o_ref, xs_ref, st_ref, coef_ref):
    tlen = u_ref.shape[0]
    half = S5_NS // 2

    @pl.when(pl.program_id(1) == 0)
    def _():
        st_ref[...] = jnp.zeros_like(st_ref)

    lanes = [(slice(p * half, (p + 1) * half), slice(S5_NS + p * half, S5_NS + (p + 1) * half))
             for p in range(2)]
    coef_ref[...] = jnp.broadcast_to(a_ref[...], coef_ref.shape)
    state = [(st_ref[:, re], st_ref[:, im]) for re, im in lanes]
    u_all = u_ref[...].reshape(tlen * S5_NB, S5_W)
    xs_ref[...] = _dot(u_all.astype(BF16), bblk_ref[...])
    for r in range(0, tlen * S5_NB, SUB_ROWS):
        rows = slice(r, r + SUB_ROWS)
        steps = slice(r // S5_NB, (r + SUB_ROWS) // S5_NB)
        u_sub = u_ref[steps].reshape(SUB_ROWS, S5_W)
        for t in range(r, r + SUB_ROWS, S5_NB):
            now = slice(t, t + S5_NB)
            for p, (re, im) in enumerate(lanes):
                ar, ai, (xr, xi) = coef_ref[:, re], coef_ref[:, im], state[p]
                nr = ar * xr - ai * xi + xs_ref[now, re]
                ni = ar * xi + ai * xr + xs_ref[now, im]
                xs_ref[now, re] = nr
                xs_ref[now, im] = ni
                state[p] = (nr, ni)
        y = (_dot(xs_ref[rows, 0:S5_NS].astype(BF16), cblk_ref[0:S5_NS, :])
             + _dot(xs_ref[rows, S5_NS:].astype(BF16), cblk_ref[S5_NS:, :]) + d_ref[...] * u_sub)
        y = jax.nn.gelu(y)
        y = y * jax.nn.sigmoid(_dot(y.astype(BF16), wglu_ref[...]) + bglu_ref[...])
        o_ref[steps] = y.reshape(SUB_ROWS // S5_NB, S5_NB, S5_W)
    for p, (re, im) in enumerate(lanes):
        st_ref[:, re], st_ref[:, im] = state[p]


def _s5(u_tm, bblk, a_vec, cblk, d_skip, w_glu, b_glu, tlen):
    seq, bsz, _ = u_tm.shape
    consts = [bblk, a_vec, cblk, d_skip, w_glu, b_glu]
    return pl.pallas_call(
        _s5_kernel,
        grid=(bsz // S5_NB, seq // tlen),
        in_specs=[pl.BlockSpec((tlen, S5_NB, S5_W), lambda b, t: (t, b, 0))]
                 + [_const_spec(c.shape) for c in consts],
        out_specs=pl.BlockSpec((tlen, S5_NB, S5_W), lambda b, t: (t, b, 0)),
        out_shape=jax.ShapeDtypeStruct((seq, bsz, S5_W), F32),
        scratch_shapes=[pltpu.VMEM((tlen * S5_NB, 2 * S5_NS), F32), pltpu.VMEM((S5_NB, 2 * S5_NS), F32),
                        pltpu.VMEM((S5_NB, 2 * S5_NS), F32)],
        compiler_params=_params(("parallel", "arbitrary")),
        name="s5",
    )(u_tm, *consts)


def _merge_kernel(x_ref, ret_ref, att_ref, gla_ref, s5_ref, wg_ref, bg_ref, wb_ref, wo_ref, g_ref, b_ref, o_ref):
    for r in range(0, x_ref.shape[0], SUB_ROWS):
        rows = slice(r, r + SUB_ROWS)
        x = x_ref[rows, :]
        xb = x.astype(BF16)
        branches = (ret_ref[rows, :], att_ref[rows, :], gla_ref[rows, :], s5_ref[rows, :].astype(BF16))
        mixed = None
        for i, br in enumerate(branches):
            gate = jax.nn.sigmoid(_dot(xb, wg_ref[i]) + bg_ref[i])
            term = gate * _dot(br, wb_ref[i])
            mixed = term if mixed is None else mixed + term
        y = DEEPNORM_ALPHA * x + _dot(mixed.astype(BF16), wo_ref[...])
        o_ref[rows, :] = _layer_norm(y, g_ref[...], b_ref[...])


def _merge(x, o_ret, o_att, o_gla, o_s5_tm, wg, bg, wb, wo, ln_g, ln_b, tm):
    bsz, seq, _ = x.shape
    consts = [wg, bg, wb, wo, ln_g, ln_b]
    tok = lambda w: pl.BlockSpec((None, tm, w), lambda b, t: (b, t, 0))
    return pl.pallas_call(
        _merge_kernel,
        grid=(bsz, seq // tm),
        in_specs=[tok(D_MODEL), tok(BRANCH_WIDTH), tok(BRANCH_WIDTH), tok(BRANCH_WIDTH),
                  pl.BlockSpec((tm, S5_W), lambda b, t: (t, b))]
                 + [_const_spec(c.shape) for c in consts],
        out_specs=tok(D_MODEL),
        out_shape=jax.ShapeDtypeStruct((bsz, seq, D_MODEL), F32),
        compiler_params=_params(("parallel", "parallel")),
        name="merge",
    )(x, o_ret, o_att, o_gla, o_s5_tm, *consts)


def _ffn_kernel(x_ref, wg_ref, wu_ref, wd_ref, g_ref, b_ref, o_ref):
    for r in range(0, x_ref.shape[0], SUB_ROWS):
        rows = slice(r, r + SUB_ROWS)
        x = x_ref[rows, :]
        xb = x.astype(BF16)
        gate = _dot(xb, wg_ref[...])
        act = (gate * jax.nn.sigmoid(gate) * _dot(xb, wu_ref[...])).astype(BF16)
        y = DEEPNORM_ALPHA * x + _dot(act, wd_ref[...])
        o_ref[rows, :] = _layer_norm(y, g_ref[...], b_ref[...])


def _ffn(x, wg, wu, wd, ln_g, ln_b, tm):
    bsz, seq, _ = x.shape
    consts = [wg, wu, wd, ln_g, ln_b]
    tok = pl.BlockSpec((None, tm, D_MODEL), lambda b, t: (b, t, 0))
    return pl.pallas_call(
        _ffn_kernel,
        grid=(bsz, seq // tm),
        in_specs=[tok] + [_const_spec(c.shape) for c in consts],
        out_specs=tok,
        out_shape=jax.ShapeDtypeStruct((bsz, seq, D_MODEL), F32),
        compiler_params=_params(("parallel", "parallel")),
        name="ffn",
    )(x, *consts)


def _tables(seq):
    f32 = F32
    pos = jnp.arange(seq)
    half = RET_DK // 2
    inv = 1.0 / (10000.0 ** (jnp.arange(half, dtype=f32) / half))
    ang = pos.astype(f32)[:, None] * inv[None, :]
    cos, sin = jnp.cos(ang), jnp.sin(ang)
    cos_h = jnp.concatenate([cos, cos], axis=1)
    sin_h = jnp.concatenate([-sin, sin], axis=1)
    tabs = {"rot_cos": jnp.tile(cos_h, (1, RET_HEADS)), "rot_sin": jnp.tile(sin_h, (1, RET_HEADS))}

    log_g = jnp.log(1.0 - 2.0 ** (-5.0 - jnp.arange(RET_HEADS, dtype=f32)))
    n = jnp.arange(BLK)
    dist = (n[:, None] - n[None, :]).astype(f32)
    cn, cm = n[:, None] // CHUNK, n[None, :] // CHUNK
    expo = jnp.where(cn == cm, jnp.abs(dist), dist)
    dec = jnp.where((cm <= cn)[None], jnp.exp(log_g[:, None, None] * expo[None]), 0.0)
    tabs["ret_dmask"] = jnp.moveaxis(dec, 0, 1).reshape(BLK, RET_HEADS * BLK)
    lane_log_g = jnp.repeat(log_g, RET_DK)[None, :]
    nf = n.astype(f32)[:, None]
    tabs["ret_xi"] = jnp.exp(lane_log_g * (nf + 1.0))
    tabs["ret_zeta"] = jnp.exp(lane_log_g * (BLK - 1.0 - nf))
    bmask = (jnp.arange(QK_W)[:, None] // RET_DK == jnp.arange(BRANCH_WIDTH)[None, :] // RET_DV).astype(f32)
    tabs["ret_bmask"] = bmask
    tabs["ret_gdec"] = jnp.exp(lane_log_g * BLK).reshape(QK_W, 1) * bmask
    hv = jnp.arange(BRANCH_WIDTH) // RET_DV
    tabs["avg"] = ((hv[:, None] == hv[None, :]).astype(f32) / RET_DV).astype(BF16)

    same = cn == cm
    tabs["gla_ltri"] = (same & (n[None, :] <= n[:, None])).astype(BF16)
    tabs["gla_bones"] = same.astype(BF16)
    ecol = jnp.arange((BLK // CHUNK) * BRANCH_WIDTH)[None, :] // BRANCH_WIDTH
    tabs["gla_esel"] = (n[:, None] // CHUNK == ecol).astype(BF16)
    return tabs


def _att_bias(rel_bias):
    rb = rel_bias.astype(F32)
    far = ATT_LEFT - ATT_MAX_REL + CHUNK
    near = rb[:, 2 * ATT_MAX_REL - 1::-1][:, :ATT_BAND + CHUNK - 1 - far]
    ext = jnp.concatenate([jnp.broadcast_to(rb[:, 2 * ATT_MAX_REL:], (ATT_HEADS, far)), near], axis=1)
    rows = [ext[:, CHUNK - 1 - q:CHUNK - 1 - q + ATT_BAND] for q in range(CHUNK)]
    return jnp.stack(rows, axis=1).reshape(ATT_HEADS * CHUNK, ATT_BAND)


def _inproj_weights(w):
    o = _IN_OFF
    col = lambda i, width: w[:, o[i]:o[i] + width]

    def swap(m):
        return m.reshape(D_MODEL, RET_HEADS, 2, RET_DK // 2)[:, :, ::-1, :].reshape(D_MODEL, QK_W)

    rq, rk = col(0, 128), col(1, 128)
    gate_pad = jnp.zeros((D_MODEL, LANE - GLA_GATE_RANK), w.dtype)
    slabs = [rq, swap(rq), rk, swap(rk), col(2, 256), col(3, 256),
             col(4, 256), col(5, 256), col(6, 256),
             col(7, 128), col(8, 128), col(9, 256), col(10, 256), col(11, GLA_GATE_RANK), gate_pad,
             col(12, 256)]
    return jnp.concatenate(slabs, axis=1).astype(BF16)


def _s5_matrices(lam_re, lam_im, log_dt, b_re, b_im, c_re, c_im):
    f32 = F32
    lr, li = lam_re.astype(f32), lam_im.astype(f32)
    dt = jnp.exp(log_dt.astype(f32))[:, None]
    mag = jnp.exp(lr * dt)
    ab_re, ab_im = mag * jnp.cos(li * dt), mag * jnp.sin(li * dt)
    den = lr * lr + li * li
    nr, ni = ab_re - 1.0, ab_im
    coef_re = (nr * lr + ni * li) / den
    coef_im = (ni * lr - nr * li) / den
    br, bi = b_re.astype(f32), b_im.astype(f32)
    bb_re = coef_re[..., None] * br - coef_im[..., None] * bi
    bb_im = coef_re[..., None] * bi + coef_im[..., None] * br
    eye = jnp.eye(S5_GROUPS, dtype=f32)

    def in_blockdiag(m):
        return jnp.einsum('gpi,gh->gihp', m, eye).reshape(S5_W, S5_NS)

    def out_blockdiag(m):
        return jnp.einsum('gip,gh->gphi', m, eye).reshape(S5_NS, S5_W)

    bblk = jnp.concatenate([in_blockdiag(bb_re), in_blockdiag(bb_im)], axis=1).astype(BF16)
    cblk = jnp.concatenate([out_blockdiag(c_re.astype(f32)), -out_blockdiag(c_im.astype(f32))], axis=0).astype(BF16)
    a_vec = jnp.concatenate([ab_re.reshape(1, S5_NS), ab_im.reshape(1, S5_NS)], axis=1)
    return bblk, a_vec, cblk


def kernel(x, w_in, gla_w_a, gla_b_a, att_rel_bias, s5_lambda_re, s5_lambda_im, s5_log_dt, s5_b_re, s5_b_im, s5_c_re, s5_c_im, s5_d, s5_w_glu, s5_b_glu, w_gate, b_gate, w_branch, w_out, ln1_g, ln1_b, w_ffn_gate, w_ffn_up, w_ffn_down, ln2_g, ln2_b):
    bsz, seq, _ = x.shape
    assert seq % (RET_GROUP * BLK) == 0 and seq % (ATT_GROUP * CHUNK) == 0 and bsz % S5_NB == 0
    tm = min(1024, seq)
    s5_tlen = min(128, seq)
    tabs = _tables(seq)
    row = lambda v: v.astype(F32).reshape(1, -1)
    for l in range(DEPTH):
        w_cat = _inproj_weights(w_in[l])
        ret_in, att_in, gla_in, s5_in = _inproj(x, w_cat, tm)
        o_ret = _retention(ret_in, tabs)
        o_att = _attention(att_in, _att_bias(att_rel_bias[l]))
        wa_pad = jnp.concatenate(
            [gla_w_a[l], jnp.zeros((LANE - GLA_GATE_RANK, QK_W), gla_w_a.dtype)], axis=0).astype(BF16)
        o_gla = _gla(gla_in, wa_pad, row(gla_b_a[l]), tabs)
        bblk, a_vec, cblk = _s5_matrices(s5_lambda_re[l], s5_lambda_im[l], s5_log_dt[l], s5_b_re[l], s5_b_im[l],
                                         s5_c_re[l], s5_c_im[l])
        o_s5 = _s5(s5_in.reshape(seq, bsz, S5_W), bblk, a_vec, cblk, row(s5_d[l]),
                   s5_w_glu[l].astype(BF16), row(s5_b_glu[l]), s5_tlen)
        x = _merge(x, o_ret, o_att, o_gla, o_s5.reshape(seq, bsz * S5_W),
                   w_gate[l].astype(BF16), b_gate[l].astype(F32).reshape(N_BRANCH, 1, D_MODEL),
                   w_branch[l].astype(BF16), w_out[l].astype(BF16), row(ln1_g[l]), row(ln1_b[l]), tm)
        x = _ffn(x, w_ffn_gate[l].astype(BF16), w_ffn_up[l].astype(BF16), w_ffn_down[l].astype(BF16),
                 row(ln2_g[l]), row(ln2_b[l]), tm)
    return x
```

```python
import jax
import jax.numpy as jnp
from jax import lax
from jax.experimental import pallas as pl
from jax.experimental.pallas import tpu as pltpu

F32 = jnp.float32
BF16 = jnp.bfloat16

D_MODEL = 1024
DEPTH = 4
CHUNK = 64
N_BRANCH = 4
BRANCH_WIDTH = D_MODEL // N_BRANCH
RET_HEADS = 4
RET_DV = BRANCH_WIDTH // RET_HEADS
RET_DK = RET_DV // 2
ATT_HEADS = 4
ATT_DH = BRANCH_WIDTH // ATT_HEADS
ATT_LEFT_CHUNKS = 8
ATT_MAX_REL = 128
GLA_HEADS = 4
GLA_DV = BRANCH_WIDTH // GLA_HEADS
GLA_DK = GLA_DV // 2
GLA_GATE_RANK = 16
GLA_GATE_TAU = 16.0
S5_GROUP = 16
S5_GROUPS = BRANCH_WIDTH // S5_GROUP
S5_STATE = 64
D_FF = ((8 * D_MODEL // 3 + 255) // 256) * 256
DEEPNORM_ALPHA = (2.0 * DEPTH) ** 0.25
LN_EPS = 1e-5

_IN_SIZES = (128, 128, 256, 256, 256, 256, 256, 128, 128, 256, 256, GLA_GATE_RANK, 256)
_IN_OFF = tuple(sum(_IN_SIZES[:i]) for i in range(len(_IN_SIZES)))

RET_W = 1024
ATT_W = 768
GLA_W = 896
S5_W = 256
LANE = 128
QK_W = RET_HEADS * RET_DK

BLK = 4 * CHUNK
ATT_LEFT = ATT_LEFT_CHUNKS * CHUNK
ATT_BAND = ATT_LEFT + CHUNK
ATT_GROUP = 4
RET_GROUP = 4
S5_NB = 8
S5_NS = S5_GROUPS * S5_STATE
VMEM_LIMIT = 56 * 1024 * 1024
SUB_ROWS = 256


def _dot(a, b):
    return jnp.dot(a, b, preferred_element_type=F32)


def _dot_nt(a, b):
    return lax.dot_general(a, b, (((1,), (1,)), ((), ())), preferred_element_type=F32)


def _split_dot(a, b):
    hi = a.astype(BF16)
    lo = (a - hi.astype(F32)).astype(BF16)
    return _dot(hi, b) + _dot(lo, b)


def _split_dot_left(a, b):
    hi = b.astype(BF16)
    lo = (b - hi.astype(F32)).astype(BF16)
    return _dot(a, hi) + _dot(a, lo)


def _layer_norm(y, g, b):
    mu = jnp.mean(y, axis=-1, keepdims=True)
    d = y - mu
    var = jnp.mean(d * d, axis=-1, keepdims=True)
    return d * lax.rsqrt(var + LN_EPS) * g + b


def _head_norm(o, avg):
    mu = _dot(o.astype(BF16), avg)
    d = o - mu
    var = _dot((d * d).astype(BF16), avg)
    return d * lax.rsqrt(var + LN_EPS)


def _params(sem):
    return pltpu.CompilerParams(dimension_semantics=sem, vmem_limit_bytes=VMEM_LIMIT)


def _const_spec(shape):
    nd = len(shape)
    return pl.BlockSpec(shape, lambda *_: (0,) * nd, pipeline_mode=pl.Buffered(1))


def _inproj_kernel(x_ref, w_ref, ret_ref, att_ref, gla_ref, s5_ref):
    ts = x_ref.shape[1]
    for p in range(0, S5_NB, 2):
        xb = x_ref[p:p + 2].reshape(2 * ts, D_MODEL).astype(BF16)
        ret_ref[p:p + 2] = _dot(xb, w_ref[:, 0:RET_W]).astype(BF16).reshape(2, ts, RET_W)
        att_ref[p:p + 2] = _dot(xb, w_ref[:, RET_W:RET_W + ATT_W]).astype(BF16).reshape(2, ts, ATT_W)
        gla_ref[p:p + 2] = _dot(
            xb, w_ref[:, RET_W + ATT_W:RET_W + ATT_W + GLA_W]).astype(BF16).reshape(2, ts, GLA_W)
        u = _dot(xb, w_ref[:, RET_W + ATT_W + GLA_W:])
        s5_ref[:, p, :] = u[0:ts]
        s5_ref[:, p + 1, :] = u[ts:2 * ts]


def _inproj(x, w_cat, ts):
    bsz, seq, _ = x.shape
    wtot = w_cat.shape[1]
    tok = lambda w: pl.BlockSpec((S5_NB, ts, w), lambda b, t: (b, t, 0))
    return pl.pallas_call(
        _inproj_kernel,
        grid=(bsz // S5_NB, seq // ts),
        in_specs=[tok(D_MODEL), _const_spec((D_MODEL, wtot))],
        out_specs=[tok(RET_W), tok(ATT_W), tok(GLA_W),
                   pl.BlockSpec((ts, S5_NB, S5_W), lambda b, t: (t, b, 0))],
        out_shape=[
            jax.ShapeDtypeStruct((bsz, seq, RET_W), BF16),
            jax.ShapeDtypeStruct((bsz, seq, ATT_W), BF16),
            jax.ShapeDtypeStruct((bsz, seq, GLA_W), BF16),
            jax.ShapeDtypeStruct((seq, bsz, S5_W), F32),
        ],
        compiler_params=_params(("parallel", "parallel")),
        name="inproj",
    )(x, w_cat)


def _ret_kernel(r_ref, cos_ref, sin_ref, dmask_ref, xi_ref, zeta_ref, gdec_ref, bmask_ref, avg_ref,
                o_ref):
    seq = r_ref.shape[0]
    lane_qk = lax.broadcasted_iota(jnp.int32, (1, QK_W), 1) // RET_DK
    lane_v = lax.broadcasted_iota(jnp.int32, (1, BRANCH_WIDTH), 1) // RET_DV

    def rotary(rows):
        cos = cos_ref[rows, :]
        sin = sin_ref[rows, :]
        q = r_ref[rows, 0:128].astype(F32) * cos + r_ref[rows, 128:256].astype(F32) * sin
        k = (r_ref[rows, 256:384].astype(F32) * cos + r_ref[rows, 384:512].astype(F32) * sin) * (RET_DK ** -0.5)
        return q, k

    def intra(rows, q, k):
        v = r_ref[rows, 512:768]
        kbd = jnp.concatenate(
            [jnp.where(lane_qk == h, k, 0.0).astype(BF16) for h in range(RET_HEADS)], axis=0)
        vbd = jnp.concatenate(
            [jnp.where(lane_v == h, v, jnp.zeros_like(v)) for h in range(RET_HEADS)], axis=0)
        s = _dot_nt(q.astype(BF16), kbd) * dmask_ref[...]
        return _dot(s.astype(BF16), vbd)

    def update(rows, k):
        kz_t = (k * zeta_ref[...]).T.astype(BF16)
        return _dot(kz_t, r_ref[rows, 512:768]) * bmask_ref[...]

    def finish(rows, o):
        g = r_ref[rows, 768:1024].astype(F32)
        o_ref[rows, :] = (g * jax.nn.sigmoid(g) * _head_norm(o, avg_ref[...])).astype(BF16)

    state = jnp.zeros((QK_W, BRANCH_WIDTH), F32)
    for j in range(seq // (RET_GROUP * BLK)):
        rows = [pl.ds((j * RET_GROUP + b) * BLK, BLK) for b in range(RET_GROUP)]
        qk = [rotary(r) for r in rows]
        o_in = [intra(r, q, k) for r, (q, k) in zip(rows, qk)]
        upd = [update(r, k) for r, (_, k) in zip(rows, qk)]
        o_x = []
        for b in range(RET_GROUP):
            o_x.append(_dot((qk[b][0] * xi_ref[...]).astype(BF16), state.astype(BF16)))
            state = gdec_ref[...] * state + upd[b]
        for b in range(RET_GROUP):
            finish(rows[b], o_in[b] + o_x[b])


def _retention(ret_in, tabs):
    bsz, seq, _ = ret_in.shape
    consts = [tabs["ret_dmask"], tabs["ret_xi"], tabs["ret_zeta"], tabs["ret_gdec"], tabs["ret_bmask"],
              tabs["avg"]]
    return pl.pallas_call(
        _ret_kernel,
        grid=(bsz,),
        in_specs=[pl.BlockSpec((None, seq, RET_W), lambda b: (b, 0, 0)),
                  _const_spec((seq, QK_W)), _const_spec((seq, QK_W))]
                 + [_const_spec(c.shape) for c in consts],
        out_specs=pl.BlockSpec((None, seq, BRANCH_WIDTH), lambda b: (b, 0, 0)),
        out_shape=jax.ShapeDtypeStruct((bsz, seq, BRANCH_WIDTH), BF16),
        compiler_params=_params(("parallel",)),
        name="retention",
    )(ret_in, tabs["rot_cos"], tabs["rot_sin"], *consts)


def _att_kernel(a_ref, bias_ref, o_ref, k_ref, v_ref):
    seq = a_ref.shape[0]
    lane_h = lax.broadcasted_iota(jnp.int32, (1, BRANCH_WIDTH), 1) // ATT_DH
    band_pos = lax.broadcasted_iota(jnp.int32, (1, ATT_BAND), 1)
    k_ref[0:ATT_LEFT, :] = jnp.zeros((ATT_LEFT, BRANCH_WIDTH), BF16)
    v_ref[0:ATT_LEFT, :] = jnp.zeros((ATT_LEFT, BRANCH_WIDTH), BF16)
    k_ref[ATT_LEFT:, :] = a_ref[:, 256:512]
    v_ref[ATT_LEFT:, :] = a_ref[:, 512:768]

    def scores(r0, padded):
        q = a_ref[pl.ds(r0, CHUNK), 0:256] * jnp.asarray(ATT_DH ** -0.5, BF16)
        qst = jnp.concatenate(
            [jnp.where(lane_h == h, q, jnp.zeros_like(q)) for h in range(ATT_HEADS)], axis=0)
        s = _dot_nt(qst, k_ref[pl.ds(r0, ATT_BAND), :]) + bias_ref[...]
        if padded:
            s = jnp.where(band_pos >= ATT_LEFT - r0, s, -1e30)
        return s

    def softmax(s):
        p = jnp.exp(s - jnp.max(s, axis=-1, keepdims=True))
        return p.astype(BF16), 1.0 / jnp.sum(p, axis=-1, keepdims=True)

    def output(r0, p, inv_l):
        pv = _dot(p, v_ref[pl.ds(r0, ATT_BAND), :]) * inv_l
        o = jnp.where(lane_h == 0, pv[0:CHUNK], 0.0)
        for h in range(1, ATT_HEADS):
            o = o + jnp.where(lane_h == h, pv[h * CHUNK:(h + 1) * CHUNK], 0.0)
        o_ref[pl.ds(r0, CHUNK), :] = o.astype(BF16)

    nchunk = seq // CHUNK
    npad = min(ATT_LEFT_CHUNKS, nchunk)
    for first in range(0, nchunk, ATT_GROUP):
        r0s = [(first + c) * CHUNK for c in range(ATT_GROUP)]
        ss = [scores(r0, first + c < npad) for c, r0 in enumerate(r0s)]
        ps = [softmax(s) for s in ss]
        for r0, (p, inv_l) in zip(r0s, ps):
            output(r0, p, inv_l)


def _attention(att_in, bias):
    bsz, seq, _ = att_in.shape
    return pl.pallas_call(
        _att_kernel,
        grid=(bsz,),
        in_specs=[pl.BlockSpec((None, seq, ATT_W), lambda b: (b, 0, 0)), _const_spec(bias.shape)],
        out_specs=pl.BlockSpec((None, seq, BRANCH_WIDTH), lambda b: (b, 0, 0)),
        out_shape=jax.ShapeDtypeStruct((bsz, seq, BRANCH_WIDTH), BF16),
        scratch_shapes=[pltpu.VMEM((ATT_LEFT + seq, BRANCH_WIDTH), BF16),
                        pltpu.VMEM((ATT_LEFT + seq, BRANCH_WIDTH), BF16)],
        compiler_params=_params(("parallel",)),
        name="attention",
    )(att_in, bias)


def _gla_kernel(g_ref, wa_ref, ba_ref, ltri_ref, bones_ref, esel_ref, bmask_ref, avg_ref,
                o_ref):
    seq = g_ref.shape[0]
    nchunk = BLK // CHUNK
    tok_chunk = lax.broadcasted_iota(jnp.int32, (1, BLK), 1) // CHUNK

    def decays(rows):
        k = g_ref[rows, 128:256].astype(F32)
        z = _dot(g_ref[rows, 768:896], wa_ref[...]) + ba_ref[...]
        log_a = (jnp.minimum(z, 0.0) - jnp.log(1.0 + jnp.exp(-jnp.abs(z)))) * (1.0 / GLA_GATE_TAU)
        cum = _split_dot_left(ltri_ref[...], log_a)
        tot = _split_dot_left(bones_ref[...], log_a)
        k_dec = k * jnp.exp(tot - cum)
        kd_t = k_dec.T.astype(BF16)
        kd_st = jnp.concatenate(
            [jnp.where(tok_chunk == c, kd_t, jnp.zeros_like(kd_t)) for c in range(nchunk)], axis=0)
        upd = _dot(kd_st, g_ref[rows, 256:512])
        gch = jnp.exp(_split_dot(log_a.T, esel_ref[...]))
        return upd, gch

    def finish(rows, o):
        r = g_ref[rows, 512:768].astype(F32)
        o = _head_norm(o * (GLA_DK ** -0.5), avg_ref[...])
        o_ref[rows, :] = (r * jax.nn.sigmoid(r) * o).astype(BF16)

    state = jnp.zeros((QK_W, BRANCH_WIDTH), F32)
    for j in range(seq // (RET_GROUP * BLK)):
        starts = [(j * RET_GROUP + b) * BLK for b in range(RET_GROUP)]
        dec = [decays(pl.ds(s, BLK)) for s in starts]
        outs = []
        for s, (upd, gch) in zip(starts, dec):
            pieces = []
            for c in range(nchunk):
                state = gch[:, c * BRANCH_WIDTH:(c + 1) * BRANCH_WIDTH] * state \
                    + upd[c * QK_W:(c + 1) * QK_W] * bmask_ref[...]
                pieces.append(_dot(g_ref[pl.ds(s + c * CHUNK, CHUNK), 0:128], state.astype(BF16)))
            outs.append(jnp.concatenate(pieces, axis=0))
        for s, o in zip(starts, outs):
            finish(pl.ds(s, BLK), o)


def _gla(gla_in, wa_pad, ba, tabs):
    bsz, seq, _ = gla_in.shape
    consts = [wa_pad, ba, tabs["gla_ltri"], tabs["gla_bones"], tabs["gla_esel"], tabs["ret_bmask"], tabs["avg"]]
    return pl.pallas_call(
        _gla_kernel,
        grid=(bsz,),
        in_specs=[pl.BlockSpec((None, seq, GLA_W), lambda b: (b, 0, 0))] + [_const_spec(c.shape) for c in consts],
        out_specs=pl.BlockSpec((None, seq, BRANCH_WIDTH), lambda b: (b, 0, 0)),
        out_shape=jax.ShapeDtypeStruct((bsz, seq, BRANCH_WIDTH), BF16),
        compiler_params=_params(("parallel",)),
        name="gla",
    )(gla_in, *consts)


def _s5_kernel(u_ref, bblk_ref, a_ref, cblk_ref, d_ref, wglu_ref, bglu_ref, o_ref, xs_ref, st_ref, coef_ref):
    tlen = u_ref.shape[0]
    half = S5_NS // 2

    @pl.when(pl.program_id(1) == 0)
    def _():
        st_ref[...] = jnp.zeros_like(st_ref)

    lanes = [(slice(p * half, (p + 1) * half), slice(S5_NS + p * half, S5_NS + (p + 1) * half))
             for p in range(2)]
    coef_ref[...] = jnp.broadcast_to(a_ref[...], coef_ref.shape)
    state = [(st_ref[:, re], st_ref[:, im]) for re, im in lanes]
    u_all = u_ref[...].reshape(tlen * S5_NB, S5_W)
    xs_ref[...] = _dot(u_all.astype(BF16), bblk_ref[...])
    for r in range(0, tlen * S5_NB, SUB_ROWS):
        rows = slice(r, r + SUB_ROWS)
        steps = slice(r // S5_NB, (r + SUB_ROWS) // S5_NB)
        u_sub = u_ref[steps].reshape(SUB_ROWS, S5_W)
        for t in range(r, r + SUB_ROWS, S5_NB):
            now = slice(t, t + S5_NB)
            for p, (re, im) in enumerate(lanes):
                ar, ai, (xr, xi) = coef_ref[:, re], coef_ref[:, im], state[p]
                nr = ar * xr - ai * xi + xs_ref[now, re]
                ni = ar * xi + ai * xr + xs_ref[now, im]
                xs_ref[now, re] = nr
                xs_ref[now, im] = ni
                state[p] = (nr, ni)
        y = _dot(xs_ref[rows, :].astype(BF16), cblk_ref[...]) + d_ref[...] * u_sub
        y = jax.nn.gelu(y)
        y = y * jax.nn.sigmoid(_dot(y.astype(BF16), wglu_ref[...]) + bglu_ref[...])
        o_ref[steps] = y.reshape(SUB_ROWS // S5_NB, S5_NB, S5_W)
    for p, (re, im) in enumerate(lanes):
        st_ref[:, re], st_ref[:, im] = state[p]


def _s5(u_tm, bblk, a_vec, cblk, d_skip, w_glu, b_glu, tlen):
    seq, bsz, _ = u_tm.shape
    consts = [bblk, a_vec, cblk, d_skip, w_glu, b_glu]
    return pl.pallas_call(
        _s5_kernel,
        grid=(bsz // S5_NB, seq // tlen),
        in_specs=[pl.BlockSpec((tlen, S5_NB, S5_W), lambda b, t: (t, b, 0))]
                 + [_const_spec(c.shape) for c in consts],
        out_specs=pl.BlockSpec((tlen, S5_NB, S5_W), lambda b, t: (t, b, 0)),
        out_shape=jax.ShapeDtypeStruct((seq, bsz, S5_W), F32),
        scratch_shapes=[pltpu.VMEM((tlen * S5_NB, 2 * S5_NS), F32), pltpu.VMEM((S5_NB, 2 * S5_NS), F32),
                        pltpu.VMEM((S5_NB, 2 * S5_NS), F32)],
        compiler_params=_params(("parallel", "arbitrary")),
        name="s5",
    )(u_tm, *consts)


def _merge_kernel(x_ref, ret_ref, att_ref, gla_ref, s5_ref, wg_ref, bg_ref, wb_ref, wo_ref, g_ref, b_ref, o_ref):
    ts = x_ref.shape[1]
    for p in range(0, S5_NB, 2):
        two = lambda ref: ref[p:p + 2].reshape(2 * ts, ref.shape[2])
        x = two(x_ref)
        xb = x.astype(BF16)
        o_s5 = jnp.concatenate([s5_ref[:, p, :], s5_ref[:, p + 1, :]], axis=0).astype(BF16)
        branches = (two(ret_ref), two(att_ref), two(gla_ref), o_s5)
        mixed = None
        for i, br in enumerate(branches):
            gate = jax.nn.sigmoid(_dot(xb, wg_ref[i]) + bg_ref[i])
            term = gate * _dot(br, wb_ref[i])
            mixed = term if mixed is None else mixed + term
        y = DEEPNORM_ALPHA * x + _dot(mixed.astype(BF16), wo_ref[...])
        o_ref[p:p + 2] = _layer_norm(y, g_ref[...], b_ref[...]).reshape(2, ts, D_MODEL)


def _merge(x, o_ret, o_att, o_gla, o_s5_tm, wg, bg, wb, wo, ln_g, ln_b, ts):
    bsz, seq, _ = x.shape
    consts = [wg, bg, wb, wo, ln_g, ln_b]
    tok = lambda w: pl.BlockSpec((S5_NB, ts, w), lambda b, t: (b, t, 0))
    return pl.pallas_call(
        _merge_kernel,
        grid=(bsz // S5_NB, seq // ts),
        in_specs=[tok(D_MODEL), tok(BRANCH_WIDTH), tok(BRANCH_WIDTH), tok(BRANCH_WIDTH),
                  pl.BlockSpec((ts, S5_NB, S5_W), lambda b, t: (t, b, 0))]
                 + [_const_spec(c.shape) for c in consts],
        out_specs=tok(D_MODEL),
        out_shape=jax.ShapeDtypeStruct((bsz, seq, D_MODEL), F32),
        compiler_params=_params(("parallel", "parallel")),
        name="merge",
    )(x, o_ret, o_att, o_gla, o_s5_tm, *consts)


def _ffn_kernel(x_ref, wg_ref, wu_ref, wd_ref, g_ref, b_ref, o_ref):
    for r in range(0, x_ref.shape[0], SUB_ROWS):
        rows = slice(r, r + SUB_ROWS)
        x = x_ref[rows, :]
        xb = x.astype(BF16)
        gate = _dot(xb, wg_ref[...])
        act = (gate * jax.nn.sigmoid(gate) * _dot(xb, wu_ref[...])).astype(BF16)
        y = DEEPNORM_ALPHA * x + _dot(act, wd_ref[...])
        o_ref[rows, :] = _layer_norm(y, g_ref[...], b_ref[...])


def _ffn(x, wg, wu, wd, ln_g, ln_b, tm):
    bsz, seq, _ = x.shape
    consts = [wg, wu, wd, ln_g, ln_b]
    tok = pl.BlockSpec((None, tm, D_MODEL), lambda b, t: (b, t, 0))
    return pl.pallas_call(
        _ffn_kernel,
        grid=(bsz, seq // tm),
        in_specs=[tok] + [_const_spec(c.shape) for c in consts],
        out_specs=tok,
        out_shape=jax.ShapeDtypeStruct((bsz, seq, D_MODEL), F32),
        compiler_params=_params(("parallel", "parallel")),
        name="ffn",
    )(x, *consts)


def _tables(seq):
    f32 = F32
    pos = jnp.arange(seq)
    half = RET_DK // 2
    inv = 1.0 / (10000.0 ** (jnp.arange(half, dtype=f32) / half))
    ang = pos.astype(f32)[:, None] * inv[None, :]
    cos, sin = jnp.cos(ang), jnp.sin(ang)
    cos_h = jnp.concatenate([cos, cos], axis=1)
    sin_h = jnp.concatenate([-sin, sin], axis=1)
    tabs = {"rot_cos": jnp.tile(cos_h, (1, RET_HEADS)), "rot_sin": jnp.tile(sin_h, (1, RET_HEADS))}

    log_g = jnp.log(1.0 - 2.0 ** (-5.0 - jnp.arange(RET_HEADS, dtype=f32)))
    n = jnp.arange(BLK)
    dist = (n[:, None] - n[None, :]).astype(f32)
    cn, cm = n[:, None] // CHUNK, n[None, :] // CHUNK
    expo = jnp.where(cn == cm, jnp.abs(dist), dist)
    dec = jnp.where((cm <= cn)[None], jnp.exp(log_g[:, None, None] * expo[None]), 0.0)
    tabs["ret_dmask"] = jnp.moveaxis(dec, 0, 1).reshape(BLK, RET_HEADS * BLK)
    lane_log_g = jnp.repeat(log_g, RET_DK)[None, :]
    nf = n.astype(f32)[:, None]
    tabs["ret_xi"] = jnp.exp(lane_log_g * (nf + 1.0))
    tabs["ret_zeta"] = jnp.exp(lane_log_g * (BLK - 1.0 - nf))
    bmask = (jnp.arange(QK_W)[:, None] // RET_DK == jnp.arange(BRANCH_WIDTH)[None, :] // RET_DV).astype(f32)
    tabs["ret_bmask"] = bmask
    tabs["ret_gdec"] = jnp.exp(lane_log_g * BLK).reshape(QK_W, 1) * bmask
    hv = jnp.arange(BRANCH_WIDTH) // RET_DV
    tabs["avg"] = ((hv[:, None] == hv[None, :]).astype(f32) / RET_DV).astype(BF16)

    same = cn == cm
    tabs["gla_ltri"] = (same & (n[None, :] <= n[:, None])).astype(BF16)
    tabs["gla_bones"] = same.astype(BF16)
    ecol = jnp.arange((BLK // CHUNK) * BRANCH_WIDTH)[None, :] // BRANCH_WIDTH
    tabs["gla_esel"] = (n[:, None] // CHUNK == ecol).astype(BF16)
    return tabs


def _att_bias(rel_bias):
    rb = rel_bias.astype(F32)
    far = ATT_LEFT - ATT_MAX_REL + CHUNK
    near = rb[:, 2 * ATT_MAX_REL - 1::-1][:, :ATT_BAND + CHUNK - 1 - far]
    ext = jnp.concatenate([jnp.broadcast_to(rb[:, 2 * ATT_MAX_REL:], (ATT_HEADS, far)), near], axis=1)
    rows = [ext[:, CHUNK - 1 - q:CHUNK - 1 - q + ATT_BAND] for q in range(CHUNK)]
    return jnp.stack(rows, axis=1).reshape(ATT_HEADS * CHUNK, ATT_BAND)


def _inproj_weights(w):
    o = _IN_OFF
    col = lambda i, width: w[:, o[i]:o[i] + width]

    def swap(m):
        return m.reshape(D_MODEL, RET_HEADS, 2, RET_DK // 2)[:, :, ::-1, :].reshape(D_MODEL, QK_W)

    rq, rk = col(0, 128), col(1, 128)
    gate_pad = jnp.zeros((D_MODEL, LANE - GLA_GATE_RANK), w.dtype)
    slabs = [rq, swap(rq), rk, swap(rk), col(2, 256), col(3, 256),
             col(4, 256), col(5, 256), col(6, 256),
             col(7, 128), col(8, 128), col(9, 256), col(10, 256), col(11, GLA_GATE_RANK), gate_pad,
             col(12, 256)]
    return jnp.concatenate(slabs, axis=1).astype(BF16)


def _s5_matrices(lam_re, lam_im, log_dt, b_re, b_im, c_re, c_im):
    f32 = F32
    lr, li = lam_re.astype(f32), lam_im.astype(f32)
    dt = jnp.exp(log_dt.astype(f32))[:, None]
    mag = jnp.exp(lr * dt)
    ab_re, ab_im = mag * jnp.cos(li * dt), mag * jnp.sin(li * dt)
    den = lr * lr + li * li
    nr, ni = ab_re - 1.0, ab_im
    coef_re = (nr * lr + ni * li) / den
    coef_im = (ni * lr - nr * li) / den
    br, bi = b_re.astype(f32), b_im.astype(f32)
    bb_re = coef_re[..., None] * br - coef_im[..., None] * bi
    bb_im = coef_re[..., None] * bi + coef_im[..., None] * br
    eye = jnp.eye(S5_GROUPS, dtype=f32)

    def in_blockdiag(m):
        return jnp.einsum('gpi,gh->gihp', m, eye).reshape(S5_W, S5_NS)

    def out_blockdiag(m):
        return jnp.einsum('gip,gh->gphi', m, eye).reshape(S5_NS, S5_W)

    bblk = jnp.concatenate([in_blockdiag(bb_re), in_blockdiag(bb_im)], axis=1).astype(BF16)
    cblk = jnp.concatenate([out_blockdiag(c_re.astype(f32)), -out_blockdiag(c_im.astype(f32))], axis=0).astype(BF16)
    a_vec = jnp.concatenate([ab_re.reshape(1, S5_NS), ab_im.reshape(1, S5_NS)], axis=1)
    return bblk, a_vec, cblk


def kernel(x, w_in, gla_w_a, gla_b_a, att_rel_bias, s5_lambda_re, s5_lambda_im, s5_log_dt, s5_b_re, s5_b_im, s5_c_re, s5_c_im, s5_d, s5_w_glu, s5_b_glu, w_gate, b_gate, w_branch, w_out, ln1_g, ln1_b, w_ffn_gate, w_ffn_up, w_ffn_down, ln2_g, ln2_b):
    bsz, seq, _ = x.shape
    assert seq % (RET_GROUP * BLK) == 0 and seq % (ATT_GROUP * CHUNK) == 0 and bsz % S5_NB == 0
    tm = min(1024, seq)
    s5_tlen = min(128, seq)
    tabs = _tables(seq)
    row = lambda v: v.astype(F32).reshape(1, -1)
    for l in range(DEPTH):
        w_cat = _inproj_weights(w_in[l])
        ret_in, att_in, gla_in, s5_in = _inproj(x, w_cat, s5_tlen)
        o_ret = _retention(ret_in, tabs)
        o_att = _attention(att_in, _att_bias(att_rel_bias[l]))
        wa_pad = jnp.concatenate(
            [gla_w_a[l], jnp.zeros((LANE - GLA_GATE_RANK, QK_W), gla_w_a.dtype)], axis=0).astype(BF16)
        o_gla = _gla(gla_in, wa_pad, row(gla_b_a[l]), tabs)
        bblk, a_vec, cblk = _s5_matrices(s5_lambda_re[l], s5_lambda_im[l], s5_log_dt[l], s5_b_re[l], s5_b_im[l],
                                         s5_c_re[l], s5_c_im[l])
        o_s5 = _s5(s5_in, bblk, a_vec, cblk, row(s5_d[l]),
                   s5_w_glu[l].astype(BF16), row(s5_b_glu[l]), s5_tlen)
        x = _merge(x, o_ret, o_att, o_gla, o_s5,
                   w_gate[l].astype(BF16), b_gate[l].astype(F32).reshape(N_BRANCH, 1, D_MODEL),
                   w_branch[l].astype(BF16), w_out[l].astype(BF16), row(ln1_g[l]), row(ln1_b[l]), s5_tlen)
        x = _ffn(x, w_ffn_gate[l].astype(BF16), w_ffn_up[l].astype(BF16), w_ffn_down[l].astype(BF16),
                 row(ln2_g[l]), row(ln2_b[l]), tm)
    return x
```

```python
import jax
import jax.numpy as jnp
from jax import lax
from jax.experimental import pallas as pl
from jax.experimental.pallas import tpu as pltpu

F32 = jnp.float32
BF16 = jnp.bfloat16

D_MODEL = 1024
DEPTH = 4
CHUNK = 64
N_BRANCH = 4
BRANCH_WIDTH = D_MODEL // N_BRANCH
RET_HEADS = 4
RET_DV = BRANCH_WIDTH // RET_HEADS
RET_DK = RET_DV // 2
ATT_HEADS = 4
ATT_DH = BRANCH_WIDTH // ATT_HEADS
ATT_LEFT_CHUNKS = 8
ATT_MAX_REL = 128
GLA_HEADS = 4
GLA_DV = BRANCH_WIDTH // GLA_HEADS
GLA_DK = GLA_DV // 2
GLA_GATE_RANK = 16
GLA_GATE_TAU = 16.0
S5_GROUP = 16
S5_GROUPS = BRANCH_WIDTH // S5_GROUP
S5_STATE = 64
D_FF = ((8 * D_MODEL // 3 + 255) // 256) * 256
DEEPNORM_ALPHA = (2.0 * DEPTH) ** 0.25
LN_EPS = 1e-5

_IN_SIZES = (128, 128, 256, 256, 256, 256, 256, 128, 128, 256, 256, GLA_GATE_RANK, 256)
_IN_OFF = tuple(sum(_IN_SIZES[:i]) for i in range(len(_IN_SIZES)))

RET_W = 1024
ATT_W = 768
GLA_W = 896
S5_W = 256
LANE = 128
QK_W = RET_HEADS * RET_DK

BLK = 4 * CHUNK
ATT_LEFT = ATT_LEFT_CHUNKS * CHUNK
ATT_BAND = ATT_LEFT + CHUNK
ATT_GROUP = 4
RET_GROUP = 4
S5_NB = 8
S5_NS = S5_GROUPS * S5_STATE
VMEM_LIMIT = 56 * 1024 * 1024
SUB_ROWS = 256
S5_SUB = 1024


def _dot(a, b):
    return jnp.dot(a, b, preferred_element_type=F32)


def _dot_nt(a, b):
    return lax.dot_general(a, b, (((1,), (1,)), ((), ())), preferred_element_type=F32)


def _split_dot_left(a, b):
    hi = b.astype(BF16)
    lo = (b - hi.astype(F32)).astype(BF16)
    return _dot(a, hi) + _dot(a, lo)


def _layer_norm(y, g, b):
    mu = jnp.mean(y, axis=-1, keepdims=True)
    d = y - mu
    var = jnp.mean(d * d, axis=-1, keepdims=True)
    return d * lax.rsqrt(var + LN_EPS) * g + b


def _head_norm(o, avg):
    mu = _dot(o.astype(BF16), avg)
    d = o - mu
    var = _dot((d * d).astype(BF16), avg)
    return d * lax.rsqrt(var + LN_EPS)


def _params(sem):
    return pltpu.CompilerParams(dimension_semantics=sem, vmem_limit_bytes=VMEM_LIMIT)


def _const_spec(shape):
    nd = len(shape)
    return pl.BlockSpec(shape, lambda *_: (0,) * nd, pipeline_mode=pl.Buffered(1))


def _inproj_kernel(x_ref, w_ref, ret_ref, att_ref, gla_ref, s5_ref):
    ts = x_ref.shape[1]
    for p in range(0, S5_NB, 2):
        xb = x_ref[p:p + 2].reshape(2 * ts, D_MODEL).astype(BF16)
        ret_ref[p:p + 2] = _dot(xb, w_ref[:, 0:RET_W]).astype(BF16).reshape(2, ts, RET_W)
        att_ref[p:p + 2] = _dot(xb, w_ref[:, RET_W:RET_W + ATT_W]).astype(BF16).reshape(2, ts, ATT_W)
        gla_ref[p:p + 2] = _dot(
            xb, w_ref[:, RET_W + ATT_W:RET_W + ATT_W + GLA_W]).astype(BF16).reshape(2, ts, GLA_W)
        u = _dot(xb, w_ref[:, RET_W + ATT_W + GLA_W:])
        s5_ref[:, p, :] = u[0:ts]
        s5_ref[:, p + 1, :] = u[ts:2 * ts]


def _inproj(x, w_cat, ts):
    bsz, seq, _ = x.shape
    wtot = w_cat.shape[1]
    tok = lambda w: pl.BlockSpec((S5_NB, ts, w), lambda b, t: (b, t, 0))
    return pl.pallas_call(
        _inproj_kernel,
        grid=(bsz // S5_NB, seq // ts),
        in_specs=[tok(D_MODEL), _const_spec((D_MODEL, wtot))],
        out_specs=[tok(RET_W), tok(ATT_W), tok(GLA_W),
                   pl.BlockSpec((ts, S5_NB, S5_W), lambda b, t: (t, b, 0))],
        out_shape=[
            jax.ShapeDtypeStruct((bsz, seq, RET_W), BF16),
            jax.ShapeDtypeStruct((bsz, seq, ATT_W), BF16),
            jax.ShapeDtypeStruct((bsz, seq, GLA_W), BF16),
            jax.ShapeDtypeStruct((seq, bsz, S5_W), F32),
        ],
        compiler_params=_params(("parallel", "parallel")),
        name="inproj",
    )(x, w_cat)


def _ret_kernel(r_ref, cos_ref, sin_ref, dmask_ref, xi_ref, zeta_ref, gdec_ref, bmask_ref, avg_ref,
                o_ref):
    seq = r_ref.shape[0]
    lane_qk = lax.broadcasted_iota(jnp.int32, (1, QK_W), 1) // RET_DK
    lane_v = lax.broadcasted_iota(jnp.int32, (1, BRANCH_WIDTH), 1) // RET_DV

    def rotary(rows):
        cos = cos_ref[rows, :]
        sin = sin_ref[rows, :]
        q = r_ref[rows, 0:128].astype(F32) * cos + r_ref[rows, 128:256].astype(F32) * sin
        k = (r_ref[rows, 256:384].astype(F32) * cos + r_ref[rows, 384:512].astype(F32) * sin) * (RET_DK ** -0.5)
        return q, k

    def intra(rows, q, k):
        v = r_ref[rows, 512:768]
        kbd = jnp.concatenate(
            [jnp.where(lane_qk == h, k, 0.0).astype(BF16) for h in range(RET_HEADS)], axis=0)
        vbd = jnp.concatenate(
            [jnp.where(lane_v == h, v, jnp.zeros_like(v)) for h in range(RET_HEADS)], axis=0)
        s = _dot_nt(q.astype(BF16), kbd) * dmask_ref[...]
        return _dot(s.astype(BF16), vbd)

    def update(rows, k):
        kz_t = (k * zeta_ref[...]).T.astype(BF16)
        return _dot(kz_t, r_ref[rows, 512:768]) * bmask_ref[...]

    def finish(rows, o):
        g = r_ref[rows, 768:1024].astype(F32)
        o_ref[rows, :] = (g * jax.nn.sigmoid(g) * _head_norm(o, avg_ref[...])).astype(BF16)

    state = jnp.zeros((QK_W, BRANCH_WIDTH), F32)
    for j in range(seq // (RET_GROUP * BLK)):
        rows = [pl.ds((j * RET_GROUP + b) * BLK, BLK) for b in range(RET_GROUP)]
        qk = [rotary(r) for r in rows]
        o_in = [intra(r, q, k) for r, (q, k) in zip(rows, qk)]
        upd = [update(r, k) for r, (_, k) in zip(rows, qk)]
        o_x = []
        for b in range(RET_GROUP):
            o_x.append(_dot((qk[b][0] * xi_ref[...]).astype(BF16), state.astype(BF16)))
            state = gdec_ref[...] * state + upd[b]
        for b in range(RET_GROUP):
            finish(rows[b], o_in[b] + o_x[b])


def _retention(ret_in, tabs):
    bsz, seq, _ = ret_in.shape
    consts = [tabs["ret_dmask"], tabs["ret_xi"], tabs["ret_zeta"], tabs["ret_gdec"], tabs["ret_bmask"],
              tabs["avg"]]
    return pl.pallas_call(
        _ret_kernel,
        grid=(bsz,),
        in_specs=[pl.BlockSpec((None, seq, RET_W), lambda b: (b, 0, 0)),
                  _const_spec((seq, QK_W)), _const_spec((seq, QK_W))]
                 + [_const_spec(c.shape) for c in consts],
        out_specs=pl.BlockSpec((None, seq, BRANCH_WIDTH), lambda b: (b, 0, 0)),
        out_shape=jax.ShapeDtypeStruct((bsz, seq, BRANCH_WIDTH), BF16),
        compiler_params=_params(("parallel",)),
        name="retention",
    )(ret_in, tabs["rot_cos"], tabs["rot_sin"], *consts)


def _att_kernel(a_ref, bias_ref, o_ref, k_ref, v_ref):
    seq = a_ref.shape[0]
    lane_h = lax.broadcasted_iota(jnp.int32, (1, BRANCH_WIDTH), 1) // ATT_DH
    band_pos = lax.broadcasted_iota(jnp.int32, (1, ATT_BAND), 1)
    k_ref[0:ATT_LEFT, :] = jnp.zeros((ATT_LEFT, BRANCH_WIDTH), BF16)
    v_ref[0:ATT_LEFT, :] = jnp.zeros((ATT_LEFT, BRANCH_WIDTH), BF16)
    k_ref[ATT_LEFT:, :] = a_ref[:, 256:512]
    v_ref[ATT_LEFT:, :] = a_ref[:, 512:768]

    def scores(r0, padded):
        q = a_ref[pl.ds(r0, CHUNK), 0:256] * jnp.asarray(ATT_DH ** -0.5, BF16)
        qst = jnp.concatenate(
            [jnp.where(lane_h == h, q, jnp.zeros_like(q)) for h in range(ATT_HEADS)], axis=0)
        s = _dot_nt(qst, k_ref[pl.ds(r0, ATT_BAND), :]) + bias_ref[...]
        if padded:
            s = jnp.where(band_pos >= ATT_LEFT - r0, s, -1e30)
        return s

    def softmax(s):
        p = jnp.exp(s - jnp.max(s, axis=-1, keepdims=True))
        return p.astype(BF16), 1.0 / jnp.sum(p, axis=-1, keepdims=True)

    def output(r0, p, inv_l):
        pv = _dot(p, v_ref[pl.ds(r0, ATT_BAND), :]) * inv_l
        o = jnp.where(lane_h == 0, pv[0:CHUNK], 0.0)
        for h in range(1, ATT_HEADS):
            o = o + jnp.where(lane_h == h, pv[h * CHUNK:(h + 1) * CHUNK], 0.0)
        o_ref[pl.ds(r0, CHUNK), :] = o.astype(BF16)

    nchunk = seq // CHUNK
    npad = min(ATT_LEFT_CHUNKS, nchunk)
    for first in range(0, nchunk, ATT_GROUP):
        r0s = [(first + c) * CHUNK for c in range(ATT_GROUP)]
        ss = [scores(r0, first + c < npad) for c, r0 in enumerate(r0s)]
        ps = [softmax(s) for s in ss]
        for r0, (p, inv_l) in zip(r0s, ps):
            output(r0, p, inv_l)


def _attention(att_in, bias):
    bsz, seq, _ = att_in.shape
    return pl.pallas_call(
        _att_kernel,
        grid=(bsz,),
        in_specs=[pl.BlockSpec((None, seq, ATT_W), lambda b: (b, 0, 0)), _const_spec(bias.shape)],
        out_specs=pl.BlockSpec((None, seq, BRANCH_WIDTH), lambda b: (b, 0, 0)),
        out_shape=jax.ShapeDtypeStruct((bsz, seq, BRANCH_WIDTH), BF16),
        scratch_shapes=[pltpu.VMEM((ATT_LEFT + seq, BRANCH_WIDTH), BF16),
                        pltpu.VMEM((ATT_LEFT + seq, BRANCH_WIDTH), BF16)],
        compiler_params=_params(("parallel",)),
        name="attention",
    )(att_in, bias)


def _gla_kernel(g_ref, wa_ref, ba_ref, ltri_ref, bmask_ref, avg_ref, o_ref):
    seq = g_ref.shape[0]
    nchunk = BLK // CHUNK
    tok_chunk = lax.broadcasted_iota(jnp.int32, (1, BLK), 1) // CHUNK
    eye = (lax.broadcasted_iota(jnp.int32, (QK_W, QK_W), 0)
           == lax.broadcasted_iota(jnp.int32, (QK_W, QK_W), 1))

    def decays(rows):
        k = g_ref[rows, 128:256].astype(F32)
        z = _dot(g_ref[rows, 768:896], wa_ref[...]) + ba_ref[...]
        log_a = (jnp.minimum(z, 0.0) - jnp.log(1.0 + jnp.exp(-jnp.abs(z)))) * (1.0 / GLA_GATE_TAU)
        cum = _split_dot_left(ltri_ref[...], log_a)
        last = [cum[(c + 1) * CHUNK - 1:(c + 1) * CHUNK, :] for c in range(nchunk)]
        tot = jnp.concatenate([jnp.broadcast_to(t, (CHUNK, QK_W)) for t in last], axis=0)
        k_dec = k * jnp.exp(tot - cum)
        kd_t = k_dec.T.astype(BF16)
        kd_st = jnp.concatenate(
            [jnp.where(tok_chunk == c, kd_t, jnp.zeros_like(kd_t)) for c in range(nchunk)], axis=0)
        upd = _dot(kd_st, g_ref[rows, 256:512])
        gch = [jnp.exp(jnp.sum(jnp.where(eye, jnp.broadcast_to(t, (QK_W, QK_W)), 0.0), axis=1, keepdims=True))
               for t in last]
        return upd, gch

    def finish(rows, o):
        r = g_ref[rows, 512:768].astype(F32)
        o = _head_norm(o * (GLA_DK ** -0.5), avg_ref[...])
        o_ref[rows, :] = (r * jax.nn.sigmoid(r) * o).astype(BF16)

    state = jnp.zeros((QK_W, BRANCH_WIDTH), F32)
    for j in range(seq // (RET_GROUP * BLK)):
        starts = [(j * RET_GROUP + b) * BLK for b in range(RET_GROUP)]
        dec = [decays(pl.ds(s, BLK)) for s in starts]
        outs = []
        for s, (upd, gch) in zip(starts, dec):
            pieces = []
            for c in range(nchunk):
                state = gch[c] * state + upd[c * QK_W:(c + 1) * QK_W] * bmask_ref[...]
                pieces.append(_dot(g_ref[pl.ds(s + c * CHUNK, CHUNK), 0:128], state.astype(BF16)))
            outs.append(jnp.concatenate(pieces, axis=0))
        for s, o in zip(starts, outs):
            finish(pl.ds(s, BLK), o)


def _gla(gla_in, wa_pad, ba, tabs):
    bsz, seq, _ = gla_in.shape
    consts = [wa_pad, ba, tabs["gla_ltri"], tabs["ret_bmask"], tabs["avg"]]
    return pl.pallas_call(
        _gla_kernel,
        grid=(bsz,),
        in_specs=[pl.BlockSpec((None, seq, GLA_W), lambda b: (b, 0, 0))] + [_const_spec(c.shape) for c in consts],
        out_specs=pl.BlockSpec((None, seq, BRANCH_WIDTH), lambda b: (b, 0, 0)),
        out_shape=jax.ShapeDtypeStruct((bsz, seq, BRANCH_WIDTH), BF16),
        compiler_params=_params(("parallel",)),
        name="gla",
    )(gla_in, *consts)


def _s5_kernel(u_ref, bblk_ref, a_ref, cblk_ref, d_ref, wglu_ref, bglu_ref, o_ref, xs_ref, st_ref, coef_ref):
    tlen = u_ref.shape[0]
    half = S5_NS // 2

    @pl.when(pl.program_id(1) == 0)
    def _():
        st_ref[...] = jnp.zeros_like(st_ref)

    lanes = [(slice(p * half, (p + 1) * half), slice(S5_NS + p * half, S5_NS + (p + 1) * half))
             for p in range(2)]
    coef_ref[...] = jnp.broadcast_to(a_ref[...], coef_ref.shape)
    state = [(st_ref[:, re], st_ref[:, im]) for re, im in lanes]
    u_all = u_ref[...].reshape(tlen * S5_NB, S5_W)
    xs_ref[...] = _dot(u_all.astype(BF16), bblk_ref[...])
    for r in range(0, tlen * S5_NB, S5_SUB):
        rows = slice(r, r + S5_SUB)
        steps = slice(r // S5_NB, (r + S5_SUB) // S5_NB)
        u_sub = u_ref[steps].reshape(S5_SUB, S5_W)
        for t in range(r, r + S5_SUB, S5_NB):
            now = slice(t, t + S5_NB)
            for p, (re, im) in enumerate(lanes):
                ar, ai, (xr, xi) = coef_ref[:, re], coef_ref[:, im], state[p]
                nr = ar * xr - ai * xi + xs_ref[now, re]
                ni = ar * xi + ai * xr + xs_ref[now, im]
                xs_ref[now, re] = nr
                xs_ref[now, im] = ni
                state[p] = (nr, ni)
        y = _dot(xs_ref[rows, :].astype(BF16), cblk_ref[...]) + d_ref[...] * u_sub
        y = jax.nn.gelu(y)
        y = y * jax.nn.sigmoid(_dot(y.astype(BF16), wglu_ref[...]) + bglu_ref[...])
        o_ref[steps] = y.reshape(S5_SUB // S5_NB, S5_NB, S5_W)
    for p, (re, im) in enumerate(lanes):
        st_ref[:, re], st_ref[:, im] = state[p]


def _s5(u_tm, bblk, a_vec, cblk, d_skip, w_glu, b_glu, tlen):
    seq, bsz, _ = u_tm.shape
    consts = [bblk, a_vec, cblk, d_skip, w_glu, b_glu]
    return pl.pallas_call(
        _s5_kernel,
        grid=(bsz // S5_NB, seq // tlen),
        in_specs=[pl.BlockSpec((tlen, S5_NB, S5_W), lambda b, t: (t, b, 0))]
                 + [_const_spec(c.shape) for c in consts],
        out_specs=pl.BlockSpec((tlen, S5_NB, S5_W), lambda b, t: (t, b, 0)),
        out_shape=jax.ShapeDtypeStruct((seq, bsz, S5_W), F32),
        scratch_shapes=[pltpu.VMEM((tlen * S5_NB, 2 * S5_NS), F32), pltpu.VMEM((S5_NB, 2 * S5_NS), F32),
                        pltpu.VMEM((S5_NB, 2 * S5_NS), F32)],
        compiler_params=_params(("parallel", "arbitrary")),
        name="s5",
    )(u_tm, *consts)


def _merge_kernel(x_ref, ret_ref, att_ref, gla_ref, s5_ref, wg_ref, bg_ref, wb_ref, wo_ref, g_ref, b_ref, o_ref):
    ts = x_ref.shape[1]
    for p in range(0, S5_NB, 2):
        two = lambda ref: ref[p:p + 2].reshape(2 * ts, ref.shape[2])
        x = two(x_ref)
        xb = x.astype(BF16)
        o_s5 = jnp.concatenate([s5_ref[:, p, :], s5_ref[:, p + 1, :]], axis=0).astype(BF16)
        branches = (two(ret_ref), two(att_ref), two(gla_ref), o_s5)
        mixed = None
        for i, br in enumerate(branches):
            gate = jax.nn.sigmoid(_dot(xb, wg_ref[i]) + bg_ref[i])
            term = gate * _dot(br, wb_ref[i])
            mixed = term if mixed is None else mixed + term
        y = DEEPNORM_ALPHA * x + _dot(mixed.astype(BF16), wo_ref[...])
        o_ref[p:p + 2] = _layer_norm(y, g_ref[...], b_ref[...]).reshape(2, ts, D_MODEL)


def _merge(x, o_ret, o_att, o_gla, o_s5_tm, wg, bg, wb, wo, ln_g, ln_b, ts):
    bsz, seq, _ = x.shape
    consts = [wg, bg, wb, wo, ln_g, ln_b]
    tok = lambda w: pl.BlockSpec((S5_NB, ts, w), lambda b, t: (b, t, 0))
    return pl.pallas_call(
        _merge_kernel,
        grid=(bsz // S5_NB, seq // ts),
        in_specs=[tok(D_MODEL), tok(BRANCH_WIDTH), tok(BRANCH_WIDTH), tok(BRANCH_WIDTH),
                  pl.BlockSpec((ts, S5_NB, S5_W), lambda b, t: (t, b, 0))]
                 + [_const_spec(c.shape) for c in consts],
        out_specs=tok(D_MODEL),
        out_shape=jax.ShapeDtypeStruct((bsz, seq, D_MODEL), F32),
        compiler_params=_params(("parallel", "parallel")),
        name="merge",
    )(x, o_ret, o_att, o_gla, o_s5_tm, *consts)


def _ffn_kernel(x_ref, wg_ref, wu_ref, wd_ref, g_ref, b_ref, o_ref):
    for r in range(0, x_ref.shape[0], SUB_ROWS):
        rows = slice(r, r + SUB_ROWS)
        x = x_ref[rows, :]
        xb = x.astype(BF16)
        gate = _dot(xb, wg_ref[...])
        act = (gate * jax.nn.sigmoid(gate) * _dot(xb, wu_ref[...])).astype(BF16)
        y = DEEPNORM_ALPHA * x + _dot(act, wd_ref[...])
        o_ref[rows, :] = _layer_norm(y, g_ref[...], b_ref[...])


def _ffn(x, wg, wu, wd, ln_g, ln_b, tm):
    bsz, seq, _ = x.shape
    consts = [wg, wu, wd, ln_g, ln_b]
    tok = pl.BlockSpec((None, tm, D_MODEL), lambda b, t: (b, t, 0))
    return pl.pallas_call(
        _ffn_kernel,
        grid=(bsz, seq // tm),
        in_specs=[tok] + [_const_spec(c.shape) for c in consts],
        out_specs=tok,
        out_shape=jax.ShapeDtypeStruct((bsz, seq, D_MODEL), F32),
        compiler_params=_params(("parallel", "parallel")),
        name="ffn",
    )(x, *consts)


def _tables(seq):
    f32 = F32
    pos = jnp.arange(seq)
    half = RET_DK // 2
    inv = 1.0 / (10000.0 ** (jnp.arange(half, dtype=f32) / half))
    ang = pos.astype(f32)[:, None] * inv[None, :]
    cos, sin = jnp.cos(ang), jnp.sin(ang)
    cos_h = jnp.concatenate([cos, cos], axis=1)
    sin_h = jnp.concatenate([-sin, sin], axis=1)
    tabs = {"rot_cos": jnp.tile(cos_h, (1, RET_HEADS)), "rot_sin": jnp.tile(sin_h, (1, RET_HEADS))}

    log_g = jnp.log(1.0 - 2.0 ** (-5.0 - jnp.arange(RET_HEADS, dtype=f32)))
    n = jnp.arange(BLK)
    dist = (n[:, None] - n[None, :]).astype(f32)
    cn, cm = n[:, None] // CHUNK, n[None, :] // CHUNK
    expo = jnp.where(cn == cm, jnp.abs(dist), dist)
    dec = jnp.where((cm <= cn)[None], jnp.exp(log_g[:, None, None] * expo[None]), 0.0)
    tabs["ret_dmask"] = jnp.moveaxis(dec, 0, 1).reshape(BLK, RET_HEADS * BLK)
    lane_log_g = jnp.repeat(log_g, RET_DK)[None, :]
    nf = n.astype(f32)[:, None]
    tabs["ret_xi"] = jnp.exp(lane_log_g * (nf + 1.0))
    tabs["ret_zeta"] = jnp.exp(lane_log_g * (BLK - 1.0 - nf))
    bmask = (jnp.arange(QK_W)[:, None] // RET_DK == jnp.arange(BRANCH_WIDTH)[None, :] // RET_DV).astype(f32)
    tabs["ret_bmask"] = bmask
    tabs["ret_gdec"] = jnp.exp(lane_log_g * BLK).reshape(QK_W, 1) * bmask
    hv = jnp.arange(BRANCH_WIDTH) // RET_DV
    tabs["avg"] = ((hv[:, None] == hv[None, :]).astype(f32) / RET_DV).astype(BF16)

    tabs["gla_ltri"] = ((cn == cm) & (n[None, :] <= n[:, None])).astype(BF16)
    return tabs


def _att_bias(rel_bias):
    rb = rel_bias.astype(F32)
    far = ATT_LEFT - ATT_MAX_REL + CHUNK
    near = rb[:, 2 * ATT_MAX_REL - 1::-1][:, :ATT_BAND + CHUNK - 1 - far]
    ext = jnp.concatenate([jnp.broadcast_to(rb[:, 2 * ATT_MAX_REL:], (ATT_HEADS, far)), near], axis=1)
    rows = [ext[:, CHUNK - 1 - q:CHUNK - 1 - q + ATT_BAND] for q in range(CHUNK)]
    return jnp.stack(rows, axis=1).reshape(ATT_HEADS * CHUNK, ATT_BAND)


def _inproj_weights(w):
    o = _IN_OFF
    col = lambda i, width: w[:, o[i]:o[i] + width]

    def swap(m):
        return m.reshape(D_MODEL, RET_HEADS, 2, RET_DK // 2)[:, :, ::-1, :].reshape(D_MODEL, QK_W)

    rq, rk = col(0, 128), col(1, 128)
    gate_pad = jnp.zeros((D_MODEL, LANE - GLA_GATE_RANK), w.dtype)
    slabs = [rq, swap(rq), rk, swap(rk), col(2, 256), col(3, 256),
             col(4, 256), col(5, 256), col(6, 256),
             col(7, 128), col(8, 128), col(9, 256), col(10, 256), col(11, GLA_GATE_RANK), gate_pad,
             col(12, 256)]
    return jnp.concatenate(slabs, axis=1).astype(BF16)


def _s5_matrices(lam_re, lam_im, log_dt, b_re, b_im, c_re, c_im):
    f32 = F32
    lr, li = lam_re.astype(f32), lam_im.astype(f32)
    dt = jnp.exp(log_dt.astype(f32))[:, None]
    mag = jnp.exp(lr * dt)
    ab_re, ab_im = mag * jnp.cos(li * dt), mag * jnp.sin(li * dt)
    den = lr * lr + li * li
    nr, ni = ab_re - 1.0, ab_im
    coef_re = (nr * lr + ni * li) / den
    coef_im = (ni * lr - nr * li) / den
    br, bi = b_re.astype(f32), b_im.astype(f32)
    bb_re = coef_re[..., None] * br - coef_im[..., None] * bi
    bb_im = coef_re[..., None] * bi + coef_im[..., None] * br
    eye = jnp.eye(S5_GROUPS, dtype=f32)

    def in_blockdiag(m):
        return jnp.einsum('gpi,gh->gihp', m, eye).reshape(S5_W, S5_NS)

    def out_blockdiag(m):
        return jnp.einsum('gip,gh->gphi', m, eye).reshape(S5_NS, S5_W)

    bblk = jnp.concatenate([in_blockdiag(bb_re), in_blockdiag(bb_im)], axis=1).astype(BF16)
    cblk = jnp.concatenate([out_blockdiag(c_re.astype(f32)), -out_blockdiag(c_im.astype(f32))], axis=0).astype(BF16)
    a_vec = jnp.concatenate([ab_re.reshape(1, S5_NS), ab_im.reshape(1, S5_NS)], axis=1)
    return bblk, a_vec, cblk


def kernel(x, w_in, gla_w_a, gla_b_a, att_rel_bias, s5_lambda_re, s5_lambda_im, s5_log_dt, s5_b_re, s5_b_im, s5_c_re, s5_c_im, s5_d, s5_w_glu, s5_b_glu, w_gate, b_gate, w_branch, w_out, ln1_g, ln1_b, w_ffn_gate, w_ffn_up, w_ffn_down, ln2_g, ln2_b):
    bsz, seq, _ = x.shape
    assert seq % (RET_GROUP * BLK) == 0 and seq % (ATT_GROUP * CHUNK) == 0 and bsz % S5_NB == 0
    tm = min(1024, seq)
    s5_tlen = min(128, seq)
    tabs = _tables(seq)
    row = lambda v: v.astype(F32).reshape(1, -1)
    for l in range(DEPTH):
        w_cat = _inproj_weights(w_in[l])
        ret_in, att_in, gla_in, s5_in = _inproj(x, w_cat, s5_tlen)
        o_ret = _retention(ret_in, tabs)
        o_att = _attention(att_in, _att_bias(att_rel_bias[l]))
        wa_pad = jnp.concatenate(
            [gla_w_a[l], jnp.zeros((LANE - GLA_GATE_RANK, QK_W), gla_w_a.dtype)], axis=0).astype(BF16)
        o_gla = _gla(gla_in, wa_pad, row(gla_b_a[l]), tabs)
        bblk, a_vec, cblk = _s5_matrices(s5_lambda_re[l], s5_lambda_im[l], s5_log_dt[l], s5_b_re[l], s5_b_im[l],
                                         s5_c_re[l], s5_c_im[l])
        o_s5 = _s5(s5_in, bblk, a_vec, cblk, row(s5_d[l]),
                   s5_w_glu[l].astype(BF16), row(s5_b_glu[l]), min(2 * s5_tlen, seq))
        x = _merge(x, o_ret, o_att, o_gla, o_s5,
                   w_gate[l].astype(BF16), b_gate[l].astype(F32).reshape(N_BRANCH, 1, D_MODEL),
                   w_branch[l].astype(BF16), w_out[l].astype(BF16), row(ln1_g[l]), row(ln1_b[l]), s5_tlen)
        x = _ffn(x, w_ffn_gate[l].astype(BF16), w_ffn_up[l].astype(BF16), w_ffn_down[l].astype(BF16),
                 row(ln2_g[l]), row(ln2_b[l]), tm)
    return x
```

```python
import jax
import jax.numpy as jnp
from jax import lax
from jax.experimental import pallas as pl
from jax.experimental.pallas import tpu as pltpu

F32 = jnp.float32
BF16 = jnp.bfloat16

D_MODEL = 1024
DEPTH = 4
CHUNK = 64
N_BRANCH = 4
BRANCH_WIDTH = D_MODEL // N_BRANCH
RET_HEADS = 4
RET_DV = BRANCH_WIDTH // RET_HEADS
RET_DK = RET_DV // 2
ATT_HEADS = 4
ATT_DH = BRANCH_WIDTH // ATT_HEADS
ATT_LEFT_CHUNKS = 8
ATT_MAX_REL = 128
GLA_HEADS = 4
GLA_DV = BRANCH_WIDTH // GLA_HEADS
GLA_DK = GLA_DV // 2
GLA_GATE_RANK = 16
GLA_GATE_TAU = 16.0
S5_GROUP = 16
S5_GROUPS = BRANCH_WIDTH // S5_GROUP
S5_STATE = 64
D_FF = ((8 * D_MODEL // 3 + 255) // 256) * 256
DEEPNORM_ALPHA = (2.0 * DEPTH) ** 0.25
LN_EPS = 1e-5

_IN_SIZES = (128, 128, 256, 256, 256, 256, 256, 128, 128, 256, 256, GLA_GATE_RANK, 256)
_IN_OFF = tuple(sum(_IN_SIZES[:i]) for i in range(len(_IN_SIZES)))

RET_W = 1024
ATT_W = 768
GLA_W = 896
S5_W = 256
LANE = 128
QK_W = RET_HEADS * RET_DK

BLK = 4 * CHUNK
ATT_LEFT = ATT_LEFT_CHUNKS * CHUNK
ATT_BAND = ATT_LEFT + CHUNK
ATT_GROUP = 4
RET_GROUP = 4
S5_NB = 8
S5_NS = S5_GROUPS * S5_STATE
VMEM_LIMIT = 56 * 1024 * 1024
SUB_ROWS = 256
S5_SUB = 1024


def _dot(a, b):
    return jnp.dot(a, b, preferred_element_type=F32)


def _dot_nt(a, b):
    return lax.dot_general(a, b, (((1,), (1,)), ((), ())), preferred_element_type=F32)


def _split_dot_left(a, b):
    hi = b.astype(BF16)
    lo = (b - hi.astype(F32)).astype(BF16)
    return _dot(a, hi) + _dot(a, lo)


def _layer_norm(y, g, b):
    mu = jnp.mean(y, axis=-1, keepdims=True)
    d = y - mu
    var = jnp.mean(d * d, axis=-1, keepdims=True)
    return d * lax.rsqrt(var + LN_EPS) * g + b


def _head_norm(o, avg):
    mu = _dot(o.astype(BF16), avg)
    d = o - mu
    var = _dot((d * d).astype(BF16), avg)
    return d * lax.rsqrt(var + LN_EPS)


def _params(sem):
    return pltpu.CompilerParams(dimension_semantics=sem, vmem_limit_bytes=VMEM_LIMIT)


def _const_spec(shape):
    nd = len(shape)
    return pl.BlockSpec(shape, lambda *_: (0,) * nd, pipeline_mode=pl.Buffered(1))


def _inproj_kernel(x_ref, w_ref, ret_ref, att_ref, gla_ref, s5_ref):
    ts = x_ref.shape[1]
    for p in range(0, S5_NB, 2):
        xb = x_ref[p:p + 2].reshape(2 * ts, D_MODEL).astype(BF16)
        ret_ref[p:p + 2] = _dot(xb, w_ref[:, 0:RET_W]).astype(BF16).reshape(2, ts, RET_W)
        att_ref[p:p + 2] = _dot(xb, w_ref[:, RET_W:RET_W + ATT_W]).astype(BF16).reshape(2, ts, ATT_W)
        gla_ref[p:p + 2] = _dot(
            xb, w_ref[:, RET_W + ATT_W:RET_W + ATT_W + GLA_W]).astype(BF16).reshape(2, ts, GLA_W)
        u = _dot(xb, w_ref[:, RET_W + ATT_W + GLA_W:])
        s5_ref[:, p, :] = u[0:ts]
        s5_ref[:, p + 1, :] = u[ts:2 * ts]


def _inproj(x, w_cat, ts):
    bsz, seq, _ = x.shape
    wtot = w_cat.shape[1]
    tok = lambda w: pl.BlockSpec((S5_NB, ts, w), lambda b, t: (b, t, 0))
    return pl.pallas_call(
        _inproj_kernel,
        grid=(bsz // S5_NB, seq // ts),
        in_specs=[tok(D_MODEL), _const_spec((D_MODEL, wtot))],
        out_specs=[tok(RET_W), tok(ATT_W), tok(GLA_W),
                   pl.BlockSpec((ts, S5_NB, S5_W), lambda b, t: (t, b, 0))],
        out_shape=[
            jax.ShapeDtypeStruct((bsz, seq, RET_W), BF16),
            jax.ShapeDtypeStruct((bsz, seq, ATT_W), BF16),
            jax.ShapeDtypeStruct((bsz, seq, GLA_W), BF16),
            jax.ShapeDtypeStruct((seq, bsz, S5_W), F32),
        ],
        compiler_params=_params(("parallel", "parallel")),
        name="inproj",
    )(x, w_cat)


def _ret_kernel(r_ref, cos_ref, sin_ref, dmask_ref, xi_ref, zeta_ref, gdec_ref, bmask_ref, avg_ref,
                o_ref):
    seq = r_ref.shape[0]
    lane_qk = lax.broadcasted_iota(jnp.int32, (1, QK_W), 1) // RET_DK
    lane_v = lax.broadcasted_iota(jnp.int32, (1, BRANCH_WIDTH), 1) // RET_DV

    def rotary(rows):
        cos = cos_ref[rows, :]
        sin = sin_ref[rows, :]
        q = r_ref[rows, 0:128].astype(F32) * cos + r_ref[rows, 128:256].astype(F32) * sin
        k = (r_ref[rows, 256:384].astype(F32) * cos + r_ref[rows, 384:512].astype(F32) * sin) * (RET_DK ** -0.5)
        return q, k

    def intra(rows, q, k):
        v = r_ref[rows, 512:768]
        kbd = jnp.concatenate(
            [jnp.where(lane_qk == h, k, 0.0).astype(BF16) for h in range(RET_HEADS)], axis=0)
        vbd = jnp.concatenate(
            [jnp.where(lane_v == h, v, jnp.zeros_like(v)) for h in range(RET_HEADS)], axis=0)
        s = _dot_nt(q.astype(BF16), kbd) * dmask_ref[...]
        return _dot(s.astype(BF16), vbd)

    def update(rows, k):
        kz_t = (k * zeta_ref[...]).T.astype(BF16)
        return _dot(kz_t, r_ref[rows, 512:768]) * bmask_ref[...]

    def finish(rows, o):
        g = r_ref[rows, 768:1024].astype(F32)
        o_ref[rows, :] = (g * jax.nn.sigmoid(g) * _head_norm(o, avg_ref[...])).astype(BF16)

    state = jnp.zeros((QK_W, BRANCH_WIDTH), F32)
    for j in range(seq // (RET_GROUP * BLK)):
        rows = [pl.ds((j * RET_GROUP + b) * BLK, BLK) for b in range(RET_GROUP)]
        qk = [rotary(r) for r in rows]
        o_in = [intra(r, q, k) for r, (q, k) in zip(rows, qk)]
        upd = [update(r, k) for r, (_, k) in zip(rows, qk)]
        o_x = []
        for b in range(RET_GROUP):
            o_x.append(_dot((qk[b][0] * xi_ref[...]).astype(BF16), state.astype(BF16)))
            state = gdec_ref[...] * state + upd[b]
        for b in range(RET_GROUP):
            finish(rows[b], o_in[b] + o_x[b])


def _retention(ret_in, tabs):
    bsz, seq, _ = ret_in.shape
    consts = [tabs["ret_dmask"], tabs["ret_xi"], tabs["ret_zeta"], tabs["ret_gdec"], tabs["ret_bmask"],
              tabs["avg"]]
    return pl.pallas_call(
        _ret_kernel,
        grid=(bsz,),
        in_specs=[pl.BlockSpec((None, seq, RET_W), lambda b: (b, 0, 0)),
                  _const_spec((seq, QK_W)), _const_spec((seq, QK_W))]
                 + [_const_spec(c.shape) for c in consts],
        out_specs=pl.BlockSpec((None, seq, BRANCH_WIDTH), lambda b: (b, 0, 0)),
        out_shape=jax.ShapeDtypeStruct((bsz, seq, BRANCH_WIDTH), BF16),
        compiler_params=_params(("parallel",)),
        name="retention",
    )(ret_in, tabs["rot_cos"], tabs["rot_sin"], *consts)


def _att_kernel(a_ref, bias_ref, o_ref, k_ref, v_ref):
    seq = a_ref.shape[0]
    lane_h = lax.broadcasted_iota(jnp.int32, (1, BRANCH_WIDTH), 1) // ATT_DH
    band_pos = lax.broadcasted_iota(jnp.int32, (1, ATT_BAND), 1)
    k_ref[0:ATT_LEFT, :] = jnp.zeros((ATT_LEFT, BRANCH_WIDTH), BF16)
    v_ref[0:ATT_LEFT, :] = jnp.zeros((ATT_LEFT, BRANCH_WIDTH), BF16)
    k_ref[ATT_LEFT:, :] = a_ref[:, 256:512]
    v_ref[ATT_LEFT:, :] = a_ref[:, 512:768]

    def scores(r0, padded):
        q = a_ref[pl.ds(r0, CHUNK), 0:256] * jnp.asarray(ATT_DH ** -0.5, BF16)
        qst = jnp.concatenate(
            [jnp.where(lane_h == h, q, jnp.zeros_like(q)) for h in range(ATT_HEADS)], axis=0)
        s = _dot_nt(qst, k_ref[pl.ds(r0, ATT_BAND), :]) + bias_ref[...]
        if padded:
            s = jnp.where(band_pos >= ATT_LEFT - r0, s, -1e30)
        return s

    def softmax(s):
        p = jnp.exp(s - jnp.max(s, axis=-1, keepdims=True))
        return p.astype(BF16), 1.0 / jnp.sum(p, axis=-1, keepdims=True)

    def output(r0, p, inv_l):
        pv = _dot(p, v_ref[pl.ds(r0, ATT_BAND), :]) * inv_l
        o = jnp.where(lane_h == 0, pv[0:CHUNK], 0.0)
        for h in range(1, ATT_HEADS):
            o = o + jnp.where(lane_h == h, pv[h * CHUNK:(h + 1) * CHUNK], 0.0)
        o_ref[pl.ds(r0, CHUNK), :] = o.astype(BF16)

    nchunk = seq // CHUNK
    npad = min(ATT_LEFT_CHUNKS, nchunk)
    for first in range(0, nchunk, ATT_GROUP):
        r0s = [(first + c) * CHUNK for c in range(ATT_GROUP)]
        ss = [scores(r0, first + c < npad) for c, r0 in enumerate(r0s)]
        ps = [softmax(s) for s in ss]
        for r0, (p, inv_l) in zip(r0s, ps):
            output(r0, p, inv_l)


def _attention(att_in, bias):
    bsz, seq, _ = att_in.shape
    return pl.pallas_call(
        _att_kernel,
        grid=(bsz,),
        in_specs=[pl.BlockSpec((None, seq, ATT_W), lambda b: (b, 0, 0)), _const_spec(bias.shape)],
        out_specs=pl.BlockSpec((None, seq, BRANCH_WIDTH), lambda b: (b, 0, 0)),
        out_shape=jax.ShapeDtypeStruct((bsz, seq, BRANCH_WIDTH), BF16),
        scratch_shapes=[pltpu.VMEM((ATT_LEFT + seq, BRANCH_WIDTH), BF16),
                        pltpu.VMEM((ATT_LEFT + seq, BRANCH_WIDTH), BF16)],
        compiler_params=_params(("parallel",)),
        name="attention",
    )(att_in, bias)


def _gla_kernel(g_ref, wa_ref, ba_ref, ltri_ref, bmask_ref, avg_ref, o_ref):
    seq = g_ref.shape[0]
    nchunk = BLK // CHUNK
    tok_chunk = lax.broadcasted_iota(jnp.int32, (1, BLK), 1) // CHUNK
    eye = (lax.broadcasted_iota(jnp.int32, (QK_W, QK_W), 0)
           == lax.broadcasted_iota(jnp.int32, (QK_W, QK_W), 1))

    def decays(rows):
        k = g_ref[rows, 128:256].astype(F32)
        z = _dot(g_ref[rows, 768:896], wa_ref[...]) + ba_ref[...]
        log_a = (jnp.minimum(z, 0.0) - jnp.log(1.0 + jnp.exp(-jnp.abs(z)))) * (1.0 / GLA_GATE_TAU)
        cum = _split_dot_left(ltri_ref[...], log_a)
        last = [cum[(c + 1) * CHUNK - 1:(c + 1) * CHUNK, :] for c in range(nchunk)]
        tot = jnp.concatenate([jnp.broadcast_to(t, (CHUNK, QK_W)) for t in last], axis=0)
        k_dec = k * jnp.exp(tot - cum)
        kd_t = k_dec.T.astype(BF16)
        kd_st = jnp.concatenate(
            [jnp.where(tok_chunk == c, kd_t, jnp.zeros_like(kd_t)) for c in range(nchunk)], axis=0)
        upd = _dot(kd_st, g_ref[rows, 256:512])
        gch = [jnp.exp(jnp.sum(jnp.where(eye, jnp.broadcast_to(t, (QK_W, QK_W)), 0.0), axis=1, keepdims=True))
               for t in last]
        return upd, gch

    def finish(rows, o):
        r = g_ref[rows, 512:768].astype(F32)
        o = _head_norm(o * (GLA_DK ** -0.5), avg_ref[...])
        o_ref[rows, :] = (r * jax.nn.sigmoid(r) * o).astype(BF16)

    state = jnp.zeros((QK_W, BRANCH_WIDTH), F32)
    for j in range(seq // (RET_GROUP * BLK)):
        starts = [(j * RET_GROUP + b) * BLK for b in range(RET_GROUP)]
        dec = [decays(pl.ds(s, BLK)) for s in starts]
        outs = []
        for s, (upd, gch) in zip(starts, dec):
            pieces = []
            for c in range(nchunk):
                state = gch[c] * state + upd[c * QK_W:(c + 1) * QK_W] * bmask_ref[...]
                pieces.append(_dot(g_ref[pl.ds(s + c * CHUNK, CHUNK), 0:128], state.astype(BF16)))
            outs.append(jnp.concatenate(pieces, axis=0))
        for s, o in zip(starts, outs):
            finish(pl.ds(s, BLK), o)


def _gla(gla_in, wa_pad, ba, tabs):
    bsz, seq, _ = gla_in.shape
    consts = [wa_pad, ba, tabs["gla_ltri"], tabs["ret_bmask"], tabs["avg"]]
    return pl.pallas_call(
        _gla_kernel,
        grid=(bsz,),
        in_specs=[pl.BlockSpec((None, seq, GLA_W), lambda b: (b, 0, 0))] + [_const_spec(c.shape) for c in consts],
        out_specs=pl.BlockSpec((None, seq, BRANCH_WIDTH), lambda b: (b, 0, 0)),
        out_shape=jax.ShapeDtypeStruct((bsz, seq, BRANCH_WIDTH), BF16),
        compiler_params=_params(("parallel",)),
        name="gla",
    )(gla_in, *consts)


def _s5_kernel(u_ref, bblk_ref, a_ref, cblk_ref, d_ref, wglu_ref, bglu_ref, o_ref, xs_ref, st_ref, coef_ref,
               y_ref):
    tlen = u_ref.shape[0]
    half = S5_NS // 2

    @pl.when(pl.program_id(1) == 0)
    def _():
        st_ref[...] = jnp.zeros_like(st_ref)

    lanes = [(slice(p * half, (p + 1) * half), slice(S5_NS + p * half, S5_NS + (p + 1) * half))
             for p in range(2)]
    coef_ref[...] = jnp.broadcast_to(a_ref[...], coef_ref.shape)
    state = [(st_ref[:, re], st_ref[:, im]) for re, im in lanes]
    u_all = u_ref[...].reshape(tlen * S5_NB, S5_W)
    xs_ref[...] = _dot(u_all.astype(BF16), bblk_ref[...])
    for t in range(0, tlen * S5_NB, S5_NB):
        now = slice(t, t + S5_NB)
        for p, (re, im) in enumerate(lanes):
            ar, ai, (xr, xi) = coef_ref[:, re], coef_ref[:, im], state[p]
            nr = ar * xr - ai * xi + xs_ref[now, re]
            ni = ar * xi + ai * xr + xs_ref[now, im]
            xs_ref[now, re] = nr
            xs_ref[now, im] = ni
            state[p] = (nr, ni)
    for p, (re, im) in enumerate(lanes):
        st_ref[:, re], st_ref[:, im] = state[p]

    nsub = (tlen * S5_NB) // S5_SUB

    def readout(i, carry):
        rows = pl.ds(pl.multiple_of(i * S5_SUB, S5_SUB), S5_SUB)
        y_ref[rows, :] = _dot(xs_ref[rows, :].astype(BF16), cblk_ref[...])
        return carry

    def gate(i, carry):
        rows = pl.ds(pl.multiple_of(i * S5_SUB, S5_SUB), S5_SUB)
        steps = pl.ds(pl.multiple_of(i * (S5_SUB // S5_NB), S5_SUB // S5_NB), S5_SUB // S5_NB)
        y = y_ref[rows, :] + d_ref[...] * u_ref[steps].reshape(S5_SUB, S5_W)
        y = jax.nn.gelu(y)
        y = y * jax.nn.sigmoid(_dot(y.astype(BF16), wglu_ref[...]) + bglu_ref[...])
        o_ref[steps] = y.reshape(S5_SUB // S5_NB, S5_NB, S5_W)
        return carry

    lax.fori_loop(0, nsub, readout, 0)
    lax.fori_loop(0, nsub, gate, 0)


def _s5(u_tm, bblk, a_vec, cblk, d_skip, w_glu, b_glu, tlen):
    seq, bsz, _ = u_tm.shape
    consts = [bblk, a_vec, cblk, d_skip, w_glu, b_glu]
    return pl.pallas_call(
        _s5_kernel,
        grid=(bsz // S5_NB, seq // tlen),
        in_specs=[pl.BlockSpec((tlen, S5_NB, S5_W), lambda b, t: (t, b, 0))]
                 + [_const_spec(c.shape) for c in consts],
        out_specs=pl.BlockSpec((tlen, S5_NB, S5_W), lambda b, t: (t, b, 0)),
        out_shape=jax.ShapeDtypeStruct((seq, bsz, S5_W), F32),
        scratch_shapes=[pltpu.VMEM((tlen * S5_NB, 2 * S5_NS), F32), pltpu.VMEM((S5_NB, 2 * S5_NS), F32),
                        pltpu.VMEM((S5_NB, 2 * S5_NS), F32), pltpu.VMEM((tlen * S5_NB, S5_W), F32)],
        compiler_params=_params(("parallel", "arbitrary")),
        name="s5",
    )(u_tm, *consts)


def _merge_kernel(x_ref, ret_ref, att_ref, gla_ref, s5_ref, wg_ref, bg_ref, wb_ref, wo_ref, g_ref, b_ref, o_ref):
    ts = x_ref.shape[1]
    for p in range(0, S5_NB, 2):
        two = lambda ref: ref[p:p + 2].reshape(2 * ts, ref.shape[2])
        x = two(x_ref)
        xb = x.astype(BF16)
        o_s5 = jnp.concatenate([s5_ref[:, p, :], s5_ref[:, p + 1, :]], axis=0).astype(BF16)
        branches = (two(ret_ref), two(att_ref), two(gla_ref), o_s5)
        mixed = None
        for i, br in enumerate(branches):
            gate = jax.nn.sigmoid(_dot(xb, wg_ref[i]) + bg_ref[i])
            term = gate * _dot(br, wb_ref[i])
            mixed = term if mixed is None else mixed + term
        y = DEEPNORM_ALPHA * x + _dot(mixed.astype(BF16), wo_ref[...])
        o_ref[p:p + 2] = _layer_norm(y, g_ref[...], b_ref[...]).reshape(2, ts, D_MODEL)


def _merge(x, o_ret, o_att, o_gla, o_s5_tm, wg, bg, wb, wo, ln_g, ln_b, ts):
    bsz, seq, _ = x.shape
    consts = [wg, bg, wb, wo, ln_g, ln_b]
    tok = lambda w: pl.BlockSpec((S5_NB, ts, w), lambda b, t: (b, t, 0))
    return pl.pallas_call(
        _merge_kernel,
        grid=(bsz // S5_NB, seq // ts),
        in_specs=[tok(D_MODEL), tok(BRANCH_WIDTH), tok(BRANCH_WIDTH), tok(BRANCH_WIDTH),
                  pl.BlockSpec((ts, S5_NB, S5_W), lambda b, t: (t, b, 0))]
                 + [_const_spec(c.shape) for c in consts],
        out_specs=tok(D_MODEL),
        out_shape=jax.ShapeDtypeStruct((bsz, seq, D_MODEL), F32),
        compiler_params=_params(("parallel", "parallel")),
        name="merge",
    )(x, o_ret, o_att, o_gla, o_s5_tm, *consts)


def _ffn_kernel(x_ref, wg_ref, wu_ref, wd_ref, g_ref, b_ref, o_ref):
    for r in range(0, x_ref.shape[0], SUB_ROWS):
        rows = slice(r, r + SUB_ROWS)
        x = x_ref[rows, :]
        xb = x.astype(BF16)
        gate = _dot(xb, wg_ref[...])
        act = (gate * jax.nn.sigmoid(gate) * _dot(xb, wu_ref[...])).astype(BF16)
        y = DEEPNORM_ALPHA * x + _dot(act, wd_ref[...])
        o_ref[rows, :] = _layer_norm(y, g_ref[...], b_ref[...])


def _ffn(x, wg, wu, wd, ln_g, ln_b, tm):
    bsz, seq, _ = x.shape
    consts = [wg, wu, wd, ln_g, ln_b]
    tok = pl.BlockSpec((None, tm, D_MODEL), lambda b, t: (b, t, 0))
    return pl.pallas_call(
        _ffn_kernel,
        grid=(bsz, seq // tm),
        in_specs=[tok] + [_const_spec(c.shape) for c in consts],
        out_specs=tok,
        out_shape=jax.ShapeDtypeStruct((bsz, seq, D_MODEL), F32),
        compiler_params=_params(("parallel", "parallel")),
        name="ffn",
    )(x, *consts)


def _tables(seq):
    f32 = F32
    pos = jnp.arange(seq)
    half = RET_DK // 2
    inv = 1.0 / (10000.0 ** (jnp.arange(half, dtype=f32) / half))
    ang = pos.astype(f32)[:, None] * inv[None, :]
    cos, sin = jnp.cos(ang), jnp.sin(ang)
    cos_h = jnp.concatenate([cos, cos], axis=1)
    sin_h = jnp.concatenate([-sin, sin], axis=1)
    tabs = {"rot_cos": jnp.tile(cos_h, (1, RET_HEADS)), "rot_sin": jnp.tile(sin_h, (1, RET_HEADS))}

    log_g = jnp.log(1.0 - 2.0 ** (-5.0 - jnp.arange(RET_HEADS, dtype=f32)))
    n = jnp.arange(BLK)
    dist = (n[:, None] - n[None, :]).astype(f32)
    cn, cm = n[:, None] // CHUNK, n[None, :] // CHUNK
    expo = jnp.where(cn == cm, jnp.abs(dist), dist)
    dec = jnp.where((cm <= cn)[None], jnp.exp(log_g[:, None, None] * expo[None]), 0.0)
    tabs["ret_dmask"] = jnp.moveaxis(dec, 0, 1).reshape(BLK, RET_HEADS * BLK)
    lane_log_g = jnp.repeat(log_g, RET_DK)[None, :]
    nf = n.astype(f32)[:, None]
    tabs["ret_xi"] = jnp.exp(lane_log_g * (nf + 1.0))
    tabs["ret_zeta"] = jnp.exp(lane_log_g * (BLK - 1.0 - nf))
    bmask = (jnp.arange(QK_W)[:, None] // RET_DK == jnp.arange(BRANCH_WIDTH)[None, :] // RET_DV).astype(f32)
    tabs["ret_bmask"] = bmask
    tabs["ret_gdec"] = jnp.exp(lane_log_g * BLK).reshape(QK_W, 1) * bmask
    hv = jnp.arange(BRANCH_WIDTH) // RET_DV
    tabs["avg"] = ((hv[:, None] == hv[None, :]).astype(f32) / RET_DV).astype(BF16)

    tabs["gla_ltri"] = ((cn == cm) & (n[None, :] <= n[:, None])).astype(BF16)
    return tabs


def _att_bias(rel_bias):
    rb = rel_bias.astype(F32)
    far = ATT_LEFT - ATT_MAX_REL + CHUNK
    near = rb[:, 2 * ATT_MAX_REL - 1::-1][:, :ATT_BAND + CHUNK - 1 - far]
    ext = jnp.concatenate([jnp.broadcast_to(rb[:, 2 * ATT_MAX_REL:], (ATT_HEADS, far)), near], axis=1)
    rows = [ext[:, CHUNK - 1 - q:CHUNK - 1 - q + ATT_BAND] for q in range(CHUNK)]
    return jnp.stack(rows, axis=1).reshape(ATT_HEADS * CHUNK, ATT_BAND)


def _inproj_weights(w):
    o = _IN_OFF
    col = lambda i, width: w[:, o[i]:o[i] + width]

    def swap(m):
        return m.reshape(D_MODEL, RET_HEADS, 2, RET_DK // 2)[:, :, ::-1, :].reshape(D_MODEL, QK_W)

    rq, rk = col(0, 128), col(1, 128)
    gate_pad = jnp.zeros((D_MODEL, LANE - GLA_GATE_RANK), w.dtype)
    slabs = [rq, swap(rq), rk, swap(rk), col(2, 256), col(3, 256),
             col(4, 256), col(5, 256), col(6, 256),
             col(7, 128), col(8, 128), col(9, 256), col(10, 256), col(11, GLA_GATE_RANK), gate_pad,
             col(12, 256)]
    return jnp.concatenate(slabs, axis=1).astype(BF16)


def _s5_matrices(lam_re, lam_im, log_dt, b_re, b_im, c_re, c_im):
    f32 = F32
    lr, li = lam_re.astype(f32), lam_im.astype(f32)
    dt = jnp.exp(log_dt.astype(f32))[:, None]
    mag = jnp.exp(lr * dt)
    ab_re, ab_im = mag * jnp.cos(li * dt), mag * jnp.sin(li * dt)
    den = lr * lr + li * li
    nr, ni = ab_re - 1.0, ab_im
    coef_re = (nr * lr + ni * li) / den
    coef_im = (ni * lr - nr * li) / den
    br, bi = b_re.astype(f32), b_im.astype(f32)
    bb_re = coef_re[..., None] * br - coef_im[..., None] * bi
    bb_im = coef_re[..., None] * bi + coef_im[..., None] * br
    eye = jnp.eye(S5_GROUPS, dtype=f32)

    def in_blockdiag(m):
        return jnp.einsum('gpi,gh->gihp', m, eye).reshape(S5_W, S5_NS)

    def out_blockdiag(m):
        return jnp.einsum('gip,gh->gphi', m, eye).reshape(S5_NS, S5_W)

    bblk = jnp.concatenate([in_blockdiag(bb_re), in_blockdiag(bb_im)], axis=1).astype(BF16)
    cblk = jnp.concatenate([out_blockdiag(c_re.astype(f32)), -out_blockdiag(c_im.astype(f32))], axis=0).astype(BF16)
    a_vec = jnp.concatenate([ab_re.reshape(1, S5_NS), ab_im.reshape(1, S5_NS)], axis=1)
    return bblk, a_vec, cblk


def kernel(x, w_in, gla_w_a, gla_b_a, att_rel_bias, s5_lambda_re, s5_lambda_im, s5_log_dt, s5_b_re, s5_b_im, s5_c_re, s5_c_im, s5_d, s5_w_glu, s5_b_glu, w_gate, b_gate, w_branch, w_out, ln1_g, ln1_b, w_ffn_gate, w_ffn_up, w_ffn_down, ln2_g, ln2_b):
    bsz, seq, _ = x.shape
    assert seq % (RET_GROUP * BLK) == 0 and seq % (ATT_GROUP * CHUNK) == 0 and bsz % S5_NB == 0
    tm = min(1024, seq)
    s5_tlen = min(128, seq)
    tabs = _tables(seq)
    row = lambda v: v.astype(F32).reshape(1, -1)
    for l in range(DEPTH):
        w_cat = _inproj_weights(w_in[l])
        ret_in, att_in, gla_in, s5_in = _inproj(x, w_cat, s5_tlen)
        o_ret = _retention(ret_in, tabs)
        o_att = _attention(att_in, _att_bias(att_rel_bias[l]))
        wa_pad = jnp.concatenate(
            [gla_w_a[l], jnp.zeros((LANE - GLA_GATE_RANK, QK_W), gla_w_a.dtype)], axis=0).astype(BF16)
        o_gla = _gla(gla_in, wa_pad, row(gla_b_a[l]), tabs)
        bblk, a_vec, cblk = _s5_matrices(s5_lambda_re[l], s5_lambda_im[l], s5_log_dt[l], s5_b_re[l], s5_b_im[l],
                                         s5_c_re[l], s5_c_im[l])
        o_s5 = _s5(s5_in, bblk, a_vec, cblk, row(s5_d[l]),
                   s5_w_glu[l].astype(BF16), row(s5_b_glu[l]), min(2 * s5_tlen, seq))
        x = _merge(x, o_ret, o_att, o_gla, o_s5,
                   w_gate[l].astype(BF16), b_gate[l].astype(F32).reshape(N_BRANCH, 1, D_MODEL),
                   w_branch[l].astype(BF16), w_out[l].astype(BF16), row(ln1_g[l]), row(ln1_b[l]), s5_tlen)
        x = _ffn(x, w_ffn_gate[l].astype(BF16), w_ffn_up[l].astype(BF16), w_ffn_down[l].astype(BF16),
                 row(ln2_g[l]), row(ln2_b[l]), tm)
    return x
```
